```python
import math
import jax, jax.numpy as jnp
from jax import lax
import numpy as np


D_MODEL = 1024
BATCH = 2
SEQ = 8192
DEPTH = 4

HEAD_DIM = 64
N_HEADS = D_MODEL // HEAD_DIM
SWA_Q_HEADS = N_HEADS // 2
SWA_KV_HEADS = max(1, SWA_Q_HEADS // 4)
MOBA_HEADS = N_HEADS - SWA_Q_HEADS
SWA_WIDTH = SWA_Q_HEADS * HEAD_DIM
SWA_KV_WIDTH = SWA_KV_HEADS * HEAD_DIM
MOBA_WIDTH = MOBA_HEADS * HEAD_DIM
MIX_WIDTH = SWA_WIDTH + MOBA_WIDTH
IN_COLS = SWA_WIDTH + 2 * SWA_KV_WIDTH + 3 * MOBA_WIDTH
IN_SPLITS = (SWA_WIDTH,
             SWA_WIDTH + SWA_KV_WIDTH,
             SWA_WIDTH + 2 * SWA_KV_WIDTH,
             SWA_WIDTH + 2 * SWA_KV_WIDTH + MOBA_WIDTH,
             SWA_WIDTH + 2 * SWA_KV_WIDTH + 2 * MOBA_WIDTH)
SWA_WINDOW = 128
SWA_BLOCK = 128
MOBA_BLOCK = 256
MOBA_TOPK = 3
MOBA_Q_CHUNK = 128
T5_BUCKETS = 32
T5_MAX_DIST = 128
D_FF = ((8 * D_MODEL // 3 + 255) // 256) * 256
N_MOD = 9
EPS = 1e-6
NEG = -1e30

kernel_name = 'hybrid_swa_moba_macaron_trunk'


def rmsnorm(x, g):
    xf = x.astype(jnp.float32)
    y = xf * lax.rsqrt(jnp.mean(xf * xf, axis=-1, keepdims=True) + EPS)
    return (y * g.astype(jnp.float32)).astype(x.dtype)


def t5_bucket(dist):
    n = jnp.maximum(dist, 0)
    max_exact = T5_BUCKETS // 2
    nf = jnp.maximum(n, 1).astype(jnp.float32)
    large = max_exact + (jnp.log(nf / max_exact) / math.log(T5_MAX_DIST / max_exact)
                         * (T5_BUCKETS - max_exact)).astype(jnp.int32)
    large = jnp.minimum(large, T5_BUCKETS - 1)
    return jnp.where(n < max_exact, n, large)


def swiglu(h, w_gate, w_up, w_down):
    return (jax.nn.silu(h @ w_gate) * (h @ w_up)) @ w_down


def swa_attention(q, k, v, sinks, table):
    B_, S, Hq, dh = q.shape
    Hkv = k.shape[2]
    G = Hq // Hkv
    L = SWA_BLOCK
    nblk = S // L
    qb = q.reshape(B_, nblk, L, Hkv, G, dh)
    kb = k.reshape(B_, nblk, L, Hkv, dh)
    vb = v.reshape(B_, nblk, L, Hkv, dh)
    pad = ((0, 0), (1, 0), (0, 0), (0, 0), (0, 0))
    kw = jnp.concatenate([jnp.pad(kb, pad)[:, :-1], kb], axis=2)
    vw = jnp.concatenate([jnp.pad(vb, pad)[:, :-1], vb], axis=2)
    s = jnp.einsum('bnqhgd,bnkhd->bnhgqk', qb, kw).astype(jnp.float32) * (dh ** -0.5)
    qi = jnp.arange(L)[:, None]
    kj = jnp.arange(2 * L)[None, :]
    dist = qi + L - kj
    bias = table.T[:, t5_bucket(dist)].astype(jnp.float32).reshape(Hkv, G, L, 2 * L)
    kpos = jnp.arange(nblk)[:, None, None] * L - L + kj[None]
    valid = (dist >= 0)[None] & (dist < SWA_WINDOW)[None] & (kpos >= 0)
    s = jnp.where(valid[None, :, None, None], s + bias, NEG)
    sink = jnp.broadcast_to(sinks.astype(jnp.float32).reshape(1, 1, Hkv, G, 1, 1),
                            s.shape[:-1] + (1,))
    p = jax.nn.softmax(jnp.concatenate([s, sink], axis=-1), axis=-1)[..., :-1]
    o = jnp.einsum('bnhgqk,bnkhd->bnqhgd', p.astype(v.dtype), vw)
    return o.reshape(B_, S, Hq, dh)


def moba_attention(q, k, v, table):
    B_, S, H, dh = q.shape
    L = MOBA_BLOCK
    C = MOBA_Q_CHUNK
    nb = -(-S // L)
    Sp = nb * L
    padw = ((0, 0), (0, Sp - S), (0, 0), (0, 0))
    q, k, v = jnp.pad(q, padw), jnp.pad(k, padw), jnp.pad(v, padw)
    kb = k.reshape(B_, nb, L, H, dh).transpose(0, 3, 1, 2, 4)
    vb = v.reshape(B_, nb, L, H, dh).transpose(0, 3, 1, 2, 4)
    kmean = jnp.mean(kb, axis=3)
    topk = min(MOBA_TOPK, nb)
    nchunk = Sp // C
    qc = q.reshape(B_, nchunk, C, H, dh).transpose(1, 0, 3, 2, 4)
    scale = dh ** -0.5
    tableT = table.T.astype(jnp.float32)
    bi = jnp.arange(B_)[:, None, None, None]
    hi = jnp.arange(H)[None, :, None, None]
    hi5 = jnp.arange(H)[None, :, None, None, None]

    def chunk_fn(args):
        qblk, ci = args
        q_pos = ci * C + jnp.arange(C)
        cur = (ci * C) // L
        gate = jnp.einsum('bhcd,bhnd->bhcn', qblk, kmean)
        gate = jnp.where(jnp.arange(nb) < cur, gate, NEG)
        _, sel = lax.top_k(gate, topk)
        sel_ok = jnp.arange(topk) < cur
        k_sel = kb[bi, hi, sel]
        v_sel = vb[bi, hi, sel].reshape(B_, H, C, topk * L, dh)
        s_sel = jnp.einsum('bhcd,bhctld->bhctl', qblk, k_sel).astype(jnp.float32) * scale
        k_pos_sel = sel[..., None] * L + jnp.arange(L)
        dist_sel = q_pos[None, None, :, None, None] - k_pos_sel
        s_sel = s_sel + tableT[hi5, t5_bucket(dist_sel)]
        s_sel = jnp.where(sel_ok[None, None, None, :, None], s_sel, NEG)
        s_sel = s_sel.reshape(B_, H, C, topk * L)
        k_own = lax.dynamic_index_in_dim(kb, cur, axis=2, keepdims=False)
        v_own = lax.dynamic_index_in_dim(vb, cur, axis=2, keepdims=False)
        dist_own = q_pos[:, None] - (cur * L + jnp.arange(L))[None, :]
        s_own = jnp.einsum('bhcd,bhld->bhcl', qblk, k_own).astype(jnp.float32) * scale
        s_own = jnp.where(dist_own >= 0, s_own + tableT[:, t5_bucket(dist_own)], NEG)
        p = jax.nn.softmax(jnp.concatenate([s_sel, s_own], axis=-1), axis=-1).astype(v.dtype)
        o = (jnp.einsum('bhck,bhckd->bhcd', p[..., :topk * L], v_sel)
             + jnp.einsum('bhcl,bhld->bhcd', p[..., topk * L:], v_own))
        return o

    out = lax.map(chunk_fn, (qc, jnp.arange(nchunk)))
    out = out.transpose(1, 0, 3, 2, 4).reshape(B_, Sp, H, dh)
    return out[:, :S]


def hybrid_mixer(h, w_in, b_in, w_out, sinks, group_gain, rel_bias):
    B_, S, _ = h.shape
    proj = h @ w_in + b_in
    qa, ka, va, qb, kb, vb = jnp.split(proj, IN_SPLITS, axis=-1)
    qa = qa.reshape(B_, S, SWA_Q_HEADS, HEAD_DIM)
    ka = ka.reshape(B_, S, SWA_KV_HEADS, HEAD_DIM)
    va = va.reshape(B_, S, SWA_KV_HEADS, HEAD_DIM)
    qb = qb.reshape(B_, S, MOBA_HEADS, HEAD_DIM)
    kb = kb.reshape(B_, S, MOBA_HEADS, HEAD_DIM)
    vb = vb.reshape(B_, S, MOBA_HEADS, HEAD_DIM)
    ya = swa_attention(qa, ka, va, sinks, rel_bias[:, :SWA_Q_HEADS]).reshape(B_, S, SWA_WIDTH)
    yb = moba_attention(qb, kb, vb, rel_bias[:, SWA_Q_HEADS:]).reshape(B_, S, MOBA_WIDTH)
    y = jnp.concatenate([rmsnorm(ya, group_gain[:SWA_WIDTH]),
                         rmsnorm(yb, group_gain[SWA_WIDTH:])], axis=-1)
    return y @ w_out


def setup_inputs(seed: int = 0) -> dict:
    key = jax.random.key(seed)
    ks = jax.random.split(key, 16)
    f = jnp.float32

    def w(k, shape, fan_in, gain=1.0):
        return jax.random.normal(k, shape, f) * (gain * fan_in ** -0.5)

    return {
        'x': jax.random.normal(ks[0], (BATCH, SEQ, D_MODEL), f),
        'c': jax.random.normal(ks[1], (BATCH, D_MODEL), f),
        'rel_bias': 0.5 * jax.random.normal(ks[2], (T5_BUCKETS, N_HEADS), f),
        'ada_w': w(ks[3], (DEPTH, D_MODEL, N_MOD * D_MODEL), D_MODEL, 0.5),
        'ada_b': 0.01 * jax.random.normal(ks[4], (DEPTH, N_MOD * D_MODEL), f),
        'norm_pre': 1.0 + 0.05 * jax.random.normal(ks[5], (DEPTH, 3, D_MODEL), f),
        'norm_post': 1.0 + 0.05 * jax.random.normal(ks[6], (DEPTH, 3, D_MODEL), f),
        'ffn_w_gate': w(ks[7], (DEPTH, 2, D_MODEL, D_FF), D_MODEL),
        'ffn_w_up': w(ks[8], (DEPTH, 2, D_MODEL, D_FF), D_MODEL),
        'ffn_w_down': w(ks[9], (DEPTH, 2, D_FF, D_MODEL), D_FF),
        'mix_w_in': w(ks[10], (DEPTH, D_MODEL, IN_COLS), D_MODEL),
        'mix_b_in': 0.01 * jax.random.normal(ks[11], (DEPTH, IN_COLS), f),
        'mix_w_out': w(ks[12], (DEPTH, MIX_WIDTH, D_MODEL), MIX_WIDTH),
        'attn_sinks': jax.random.normal(ks[13], (DEPTH, SWA_Q_HEADS), f),
        'group_gain': 1.0 + 0.05 * jax.random.normal(ks[14], (DEPTH, MIX_WIDTH), f),
    }


def reference(x, c, rel_bias, ada_w, ada_b, norm_pre, norm_post, ffn_w_gate, ffn_w_up,
              ffn_w_down, mix_w_in, mix_b_in, mix_w_out, attn_sinks, group_gain):
    c_act = jax.nn.silu(c)
    for l in range(DEPTH):
        mod = (c_act @ ada_w[l] + ada_b[l])[:, None, :]
        sh1, sc1, g1, sh2, sc2, g2, sh3, sc3, g3 = jnp.split(mod, N_MOD, axis=-1)
        h = rmsnorm(x, norm_pre[l, 0]) * (1.0 + sc1) + sh1
        y = swiglu(h, ffn_w_gate[l, 0], ffn_w_up[l, 0], ffn_w_down[l, 0])
        x = x + 0.5 * g1 * rmsnorm(y, norm_post[l, 0])
        h = rmsnorm(x, norm_pre[l, 1]) * (1.0 + sc2) + sh2
        y = hybrid_mixer(h, mix_w_in[l], mix_b_in[l], mix_w_out[l], attn_sinks[l],
                         group_gain[l], rel_bias)
        x = x + g2 * rmsnorm(y, norm_post[l, 1])
        h = rmsnorm(x, norm_pre[l, 2]) * (1.0 + sc3) + sh3
        y = swiglu(h, ffn_w_gate[l, 1], ffn_w_up[l, 1], ffn_w_down[l, 1])
        x = x + 0.5 * g3 * rmsnorm(y, norm_post[l, 2])
    return x
```

```python
import functools
import math

import numpy as np
import jax
import jax.numpy as jnp
from jax import lax
from jax.experimental import pallas as pl
from jax.experimental.pallas import tpu as pltpu

HEAD_DIM = 64
LANES = 128
SWA_BLOCK = 128
SWA_WINDOW = 128
MOBA_BLOCK = 256
MOBA_TOPK = 3
T5_BUCKETS = 32
T5_MAX_DIST = 128
N_MOD = 9
EPS = 1e-6
NEG = -1e30
VMEM_LIMIT = 56 * 1024 * 1024

F32 = jnp.float32
BF16 = jnp.bfloat16


def _cparams(n_axes):
    return pltpu.CompilerParams(
        dimension_semantics=("arbitrary",) * n_axes,
        vmem_limit_bytes=VMEM_LIMIT,
    )


def _rms(x, g):
    return x * lax.rsqrt(jnp.mean(x * x, axis=-1, keepdims=True) + EPS) * g


def _dot(a, b):
    return jnp.dot(a, b, preferred_element_type=F32)


def _dot_nt(a, b):
    return lax.dot_general(a, b, (((1,), (1,)), ((), ())), preferred_element_type=F32)


def _mod_kernel(ct_ref, w_ref, b_ref, o_ref, *, n_batch, row_chunk):
    ct = ct_ref[...]
    ca = ct * jax.nn.sigmoid(ct)
    d = ct.shape[0]
    for b in range(n_batch):
        acc = b_ref[0]
        for r in range(0, d, row_chunk):
            w = w_ref[0, r:r + row_chunk, :]
            acc = acc + jnp.sum(w * ca[r:r + row_chunk, b:b + 1], axis=0, keepdims=True)
        o_ref[0, b:b + 1, :] = acc


def _modulation(c, ada_w, ada_b, *, tn=1152, row_chunk=128):
    depth, d, n = ada_w.shape
    n_batch = c.shape[0]
    return pl.pallas_call(
        functools.partial(_mod_kernel, n_batch=n_batch, row_chunk=row_chunk),
        grid=(depth, n // tn),
        in_specs=[
            pl.BlockSpec((d, n_batch), lambda l, j: (0, 0)),
            pl.BlockSpec((1, d, tn), lambda l, j: (l, 0, j)),
            pl.BlockSpec((1, 1, tn), lambda l, j: (l, 0, j)),
        ],
        out_specs=pl.BlockSpec((1, n_batch, tn), lambda l, j: (l, 0, j)),
        out_shape=jax.ShapeDtypeStruct((depth, n_batch, n), F32),
        compiler_params=_cparams(2),
        name="adaln_modulation",
    )(c.T, ada_w, ada_b.reshape(depth, 1, n))


def _ffn_kernel(x_ref, mod_ref, gpre_ref, gpost_ref, wg_ref, wu_ref, wd_ref, o_ref,
                *, mod_base, ff_chunk):
    x = x_ref[...]
    mod = mod_ref[0]
    shift = mod[mod_base:mod_base + 1]
    scale = mod[mod_base + 1:mod_base + 2]
    gate = mod[mod_base + 2:mod_base + 3]
    h = (_rms(x, gpre_ref[0]) * (1.0 + scale) + shift).astype(BF16)
    d_ff = wg_ref.shape[-1]
    acc = jnp.zeros(x.shape, F32)
    for c0 in range(0, d_ff, ff_chunk):
        g = _dot(h, wg_ref[0, 0, :, c0:c0 + ff_chunk])
        u = _dot(h, wu_ref[0, 0, :, c0:c0 + ff_chunk])
        a = (g * jax.nn.sigmoid(g) * u).astype(BF16)
        acc = acc + _dot(a, wd_ref[0, 0, c0:c0 + ff_chunk, :])
    o_ref[...] = x + 0.5 * gate * _rms(acc, gpost_ref[0])


def _ffn(x, mod_l, gpre, gpost, wg, wu, wd, layer, which, seq, *, tm=512, ff_chunk=1408):
    t, d = x.shape
    d_ff = wg.shape[-1]
    tiles_per_batch = seq // tm
    const = pl.Buffered(1)
    return pl.pallas_call(
        functools.partial(_ffn_kernel, mod_base=0 if which == 0 else 6, ff_chunk=ff_chunk),
        grid=(t // tm,),
        in_specs=[
            pl.BlockSpec((tm, d), lambda i: (i, 0)),
            pl.BlockSpec((1, N_MOD, d), lambda i: (i // tiles_per_batch, 0, 0)),
            pl.BlockSpec((1, 1, d), lambda i: (0, 0, 0)),
            pl.BlockSpec((1, 1, d), lambda i: (0, 0, 0)),
            pl.BlockSpec((1, 1, d, d_ff), lambda i: (layer, which, 0, 0), pipeline_mode=const),
            pl.BlockSpec((1, 1, d, d_ff), lambda i: (layer, which, 0, 0), pipeline_mode=const),
            pl.BlockSpec((1, 1, d_ff, d), lambda i: (layer, which, 0, 0), pipeline_mode=const),
        ],
        out_specs=pl.BlockSpec((tm, d), lambda i: (i, 0)),
        out_shape=jax.ShapeDtypeStruct((t, d), F32),
        compiler_params=_cparams(1),
        name="swiglu_half_step",
    )(x, mod_l, gpre, gpost, wg, wu, wd)


def _inproj_kernel(x_ref, mod_ref, gpre_ref, w_ref, b_ref, o_ref):
    x = x_ref[...]
    mod = mod_ref[0]
    h = (_rms(x, gpre_ref[0]) * (1.0 + mod[4:5]) + mod[3:4]).astype(BF16)
    o_ref[...] = (_dot(h, w_ref[0]) + b_ref[0]).astype(o_ref.dtype)


def _inproj(x, mod_l, gpre, w_in, b_in, layer, seq, *, tm=512):
    t, d = x.shape
    n = w_in.shape[-1]
    tiles_per_batch = seq // tm
    return pl.pallas_call(
        _inproj_kernel,
        grid=(t // tm,),
        in_specs=[
            pl.BlockSpec((tm, d), lambda i: (i, 0)),
            pl.BlockSpec((1, N_MOD, d), lambda i: (i // tiles_per_batch, 0, 0)),
            pl.BlockSpec((1, 1, d), lambda i: (0, 0, 0)),
            pl.BlockSpec((1, d, n), lambda i: (layer, 0, 0), pipeline_mode=pl.Buffered(1)),
            pl.BlockSpec((1, 1, n), lambda i: (layer, 0, 0)),
        ],
        out_specs=pl.BlockSpec((tm, n), lambda i: (i, 0)),
        out_shape=jax.ShapeDtypeStruct((t, n), BF16),
        compiler_params=_cparams(1),
        name="qkv_projection",
    )(x, mod_l, gpre, w_in, b_in)


def _roll_heads(x):
    return pltpu.roll(x.astype(F32), HEAD_DIM, axis=1).astype(x.dtype)


def _swa_kernel(q_ref, kp_ref, km_ref, vp_ref, vm_ref, bias_ref, sink_ref, o_ref,
                *, n_q_heads, n_kv_heads):
    tq = q_ref.shape[0]
    group = n_q_heads // n_kv_heads
    i = pl.program_id(1)
    k_all = jnp.concatenate([kp_ref[...], km_ref[...]], axis=0)
    v_all = jnp.concatenate([vp_ref[...], vm_ref[...]], axis=0)
    k_rolled = _roll_heads(k_all)
    v_rolled = _roll_heads(v_all)
    lane = lax.broadcasted_iota(jnp.int32, (SWA_BLOCK, LANES), 1)
    lo = lane < HEAD_DIM
    col = lax.broadcasted_iota(jnp.int32, (SWA_BLOCK, 2 * SWA_BLOCK), 1)
    scale = HEAD_DIM ** -0.5
    for blk in range(tq // SWA_BLOCK):
        r0 = blk * SWA_BLOCK
        kw = (k_all[r0:r0 + 2 * SWA_BLOCK], k_rolled[r0:r0 + 2 * SWA_BLOCK])
        vw = (v_all[r0:r0 + 2 * SWA_BLOCK], v_rolled[r0:r0 + 2 * SWA_BLOCK])
        for pair in range(n_q_heads // 2):
            qp = (q_ref[r0:r0 + SWA_BLOCK, pair * LANES:(pair + 1) * LANES].astype(F32)
                  * scale).astype(BF16)
            outs = []
            for odd in range(2):
                h = 2 * pair + odd
                kv = h // group
                use_roll = (kv + odd) % 2
                qm = jnp.where(lo if odd == 0 else jnp.logical_not(lo), qp, jnp.zeros_like(qp))
                s = _dot_nt(qm, kw[use_roll])
                b = bias_ref[h]
                s = jnp.where(b > 0.5 * NEG, s + b, NEG)
                if blk == 0:
                    s = jnp.where(jnp.logical_and(col < SWA_BLOCK, i == 0), NEG, s)
                sink = sink_ref[h:h + 1, 0:1]
                m = jnp.maximum(jnp.max(s, axis=1, keepdims=True), sink)
                p = jnp.exp(s - m)
                denom = jnp.sum(p, axis=1, keepdims=True) + jnp.exp(sink - m)
                outs.append(_dot(p.astype(BF16), vw[use_roll]) / denom)
            o_ref[r0:r0 + SWA_BLOCK, pair * LANES:(pair + 1) * LANES] = jnp.where(
                lo, outs[0], outs[1])


def _swa(qkv, bias, sinks, n_batch, seq, n_q_heads, n_kv_heads, *, tq=512):
    t = qkv.shape[0]
    nq = seq // tq
    qw = n_q_heads * HEAD_DIM
    k_col = qw // LANES
    v_col = k_col + (n_kv_heads * HEAD_DIM) // LANES
    per = tq // SWA_BLOCK

    def prev_map(col):
        return lambda b, i: (jnp.maximum((b * nq + i) * per - 1, 0), col)

    return pl.pallas_call(
        functools.partial(_swa_kernel, n_q_heads=n_q_heads, n_kv_heads=n_kv_heads),
        grid=(n_batch, nq),
        in_specs=[
            pl.BlockSpec((tq, qw), lambda b, i: (b * nq + i, 0)),
            pl.BlockSpec((SWA_BLOCK, LANES), prev_map(k_col)),
            pl.BlockSpec((tq, LANES), lambda b, i: (b * nq + i, k_col)),
            pl.BlockSpec((SWA_BLOCK, LANES), prev_map(v_col)),
            pl.BlockSpec((tq, LANES), lambda b, i: (b * nq + i, v_col)),
            pl.BlockSpec((n_q_heads, SWA_BLOCK, 2 * SWA_BLOCK), lambda b, i: (0, 0, 0)),
            pl.BlockSpec((n_q_heads, LANES), lambda b, i: (0, 0)),
        ],
        out_specs=pl.BlockSpec((tq, qw), lambda b, i: (b * nq + i, 0)),
        out_shape=jax.ShapeDtypeStruct((t, qw), F32),
        compiler_params=_cparams(2),
        name="swa_attention",
    )(qkv, qkv, qkv, qkv, qkv, bias, sinks)


def _moba_kernel(q_ref, k_ref, v_ref, bown_ref, bprev_ref, bconst_ref, o_ref, kmean_ref,
                 *, n_blocks):
    L = MOBA_BLOCK
    c = pl.program_id(2)

    @pl.when(c == 0)
    def _():
        kmean_ref[...] = jnp.zeros_like(kmean_ref)
        for jb in range(n_blocks):
            kb = k_ref[jb * L:(jb + 1) * L, :].astype(F32)
            kmean_ref[jb:jb + 1, :] = jnp.sum(kb, axis=0, keepdims=True) * (1.0 / L)

    scale = HEAD_DIM ** -0.5
    q = (q_ref[...].astype(F32) * scale).astype(BF16)
    lane = lax.broadcasted_iota(jnp.int32, (L, LANES), 1)
    lo = lane < HEAD_DIM
    zero = jnp.zeros_like(q)
    q2 = jnp.concatenate([jnp.where(lo, q, zero), jnp.where(lo, zero, q)], axis=0)
    rows = 2 * L
    is_even = lax.broadcasted_iota(jnp.int32, (rows, 1), 0) < L

    km = kmean_ref[...]
    km_hi = km.astype(BF16)
    km_lo = (km - km_hi.astype(F32)).astype(BF16)
    gate = _dot_nt(q2, km_hi) + _dot_nt(q2, km_lo)
    colf = lax.broadcasted_iota(jnp.int32, (rows, LANES), 1).astype(F32)
    cf = c.astype(F32)
    g = jnp.where(colf < cf, gate, NEG)
    sels = []
    for t in range(MOBA_TOPK):
        gmax = jnp.max(g, axis=1, keepdims=True)
        idx = jnp.min(jnp.where(g == gmax, colf, float(LANES)), axis=1, keepdims=True)
        sels.append(jnp.where(t < c, idx, -1.0))
        g = jnp.where(colf == idx, -jnp.inf, g)

    def selected(jf):
        hit = sels[0] == jf
        for s_t in sels[1:]:
            hit = jnp.logical_or(hit, s_t == jf)
        return hit

    bconst = jnp.where(is_even, bconst_ref[0, 0:1, 0:1], bconst_ref[0, 1:2, 0:1])

    k_own = k_ref[pl.ds(pl.multiple_of(c * L, L), L), :]
    v_own = v_ref[pl.ds(pl.multiple_of(c * L, L), L), :]
    s = _dot_nt(q2, k_own)
    b = jnp.concatenate([bown_ref[0], bown_ref[1]], axis=0)
    s = jnp.where(b > 0.5 * NEG, s + b, NEG)
    m = jnp.max(s, axis=1, keepdims=True)
    p = jnp.exp(s - m)
    l = jnp.sum(p, axis=1, keepdims=True)
    acc = _dot(p.astype(BF16), v_own)

    def visit(j, bias, carry):
        m, l, acc = carry
        start = pl.multiple_of(j * L, L)
        kj = k_ref[pl.ds(start, L), :]
        vj = v_ref[pl.ds(start, L), :]
        s = _dot_nt(q2, kj)
        s = jnp.where(selected(j.astype(F32)), s + bias, NEG)
        m_new = jnp.maximum(m, jnp.max(s, axis=1, keepdims=True))
        alpha = jnp.exp(m - m_new)
        p = jnp.exp(s - m_new)
        l = alpha * l + jnp.sum(p, axis=1, keepdims=True)
        acc = alpha * acc + _dot(p.astype(BF16), vj)
        return m_new, l, acc

    def prev_block(carry):
        bp = jnp.concatenate([bprev_ref[0], bprev_ref[1]], axis=0)
        return visit(c - 1, bp, carry)

    m, l, acc = lax.cond(c > 0, prev_block, lambda carry: carry, (m, l, acc))
    m, l, acc = lax.fori_loop(0, jnp.maximum(c - 1, 0),
                              lambda j, carry: visit(j, bconst, carry), (m, l, acc))
    out = acc / l
    o_ref[...] = jnp.where(lo, out[0:L], out[L:2 * L])


def _moba(qkv, bown, bprev, bconst, n_batch, seq, q_col, k_col, v_col, n_heads):
    t = qkv.shape[0]
    L = MOBA_BLOCK
    nb = seq // L
    n_pairs = n_heads // 2
    return pl.pallas_call(
        functools.partial(_moba_kernel, n_blocks=nb),
        grid=(n_batch, n_pairs, nb),
        in_specs=[
            pl.BlockSpec((L, LANES), lambda b, p, c: (b * nb + c, q_col + p)),
            pl.BlockSpec((seq, LANES), lambda b, p, c: (b, k_col + p)),
            pl.BlockSpec((seq, LANES), lambda b, p, c: (b, v_col + p)),
            pl.BlockSpec((2, L, L), lambda b, p, c: (p, 0, 0)),
            pl.BlockSpec((2, L, L), lambda b, p, c: (p, 0, 0)),
            pl.BlockSpec((1, 2, LANES), lambda b, p, c: (p, 0, 0)),
        ],
        out_specs=pl.BlockSpec((L, LANES), lambda b, p, c: (b * nb + c, p)),
        out_shape=jax.ShapeDtypeStruct((t, n_heads * HEAD_DIM), F32),
        scratch_shapes=[pltpu.VMEM((LANES, LANES), F32)],
        compiler_params=_cparams(3),
        name="moba_attention",
    )(qkv, qkv, qkv, bown, bprev, bconst)


def _outproj_kernel(x_ref, ya_ref, yb_ref, mod_ref, gg_ref, gpost_ref, w_ref, o_ref):
    wa = ya_ref.shape[1]
    gg = gg_ref[0]
    ya = _rms(ya_ref[...], gg[:, :wa]).astype(BF16)
    yb = _rms(yb_ref[...], gg[:, wa:]).astype(BF16)
    y = _dot(ya, w_ref[0, :wa, :]) + _dot(yb, w_ref[0, wa:, :])
    gate = mod_ref[0][5:6]
    o_ref[...] = x_ref[...] + gate * _rms(y, gpost_ref[0])


def _outproj(x, ya, yb, mod_l, gg, gpost, w_out, layer, seq, *, tm=512):
    t, d = x.shape
    wa, wb = ya.shape[1], yb.shape[1]
    tiles_per_batch = seq // tm
    return pl.pallas_call(
        _outproj_kernel,
        grid=(t // tm,),
        in_specs=[
            pl.BlockSpec((tm, d), lambda i: (i, 0)),
            pl.BlockSpec((tm, wa), lambda i: (i, 0)),
            pl.BlockSpec((tm, wb), lambda i: (i, 0)),
            pl.BlockSpec((1, N_MOD, d), lambda i: (i // tiles_per_batch, 0, 0)),
            pl.BlockSpec((1, 1, wa + wb), lambda i: (0, 0, 0)),
            pl.BlockSpec((1, 1, d), lambda i: (0, 0, 0)),
            pl.BlockSpec((1, wa + wb, d), lambda i: (layer, 0, 0), pipeline_mode=pl.Buffered(1)),
        ],
        out_specs=pl.BlockSpec((tm, d), lambda i: (i, 0)),
        out_shape=jax.ShapeDtypeStruct((t, d), F32),
        compiler_params=_cparams(1),
        name="mixer_out_projection",
    )(x, ya, yb, mod_l, gg, gpost, w_out)


def _t5_bucket_np(dist):
    n = np.maximum(dist, 0)
    max_exact = T5_BUCKETS // 2
    nf = np.maximum(n, 1).astype(np.float32)
    large = max_exact + (np.log(nf / np.float32(max_exact)) / np.float32(math.log(T5_MAX_DIST / max_exact))
                         * np.float32(T5_BUCKETS - max_exact)).astype(np.int32)
    large = np.minimum(large, T5_BUCKETS - 1)
    return np.where(n < max_exact, n, large)


def _bias_tables(rel_bias, n_swa_heads):
    qi = np.arange(SWA_BLOCK)[:, None]
    kj = np.arange(2 * SWA_BLOCK)[None, :]
    dist = qi + SWA_BLOCK - kj
    valid = (dist >= 0) & (dist < SWA_WINDOW)
    table_swa = rel_bias[:, :n_swa_heads].T
    swa = jnp.where(valid[None], table_swa[:, _t5_bucket_np(dist)], NEG)

    table_moba = rel_bias[:, n_swa_heads:].T
    i = np.arange(MOBA_BLOCK)[:, None]
    j = np.arange(MOBA_BLOCK)[None, :]
    d_own = i - j
    own = jnp.where((d_own >= 0)[None], table_moba[:, _t5_bucket_np(d_own)], NEG)
    prev = table_moba[:, _t5_bucket_np(d_own + MOBA_BLOCK)]
    far = int(_t5_bucket_np(np.array(MOBA_BLOCK + 1)))
    assert far == int(_t5_bucket_np(np.array(1 << 30)))
    const = jnp.broadcast_to(table_moba[:, far][:, None], (table_moba.shape[0], LANES))
    return swa.astype(F32), own.astype(F32), prev.astype(F32), const.reshape(-1, 2, LANES).astype(F32)


def kernel(x, c, rel_bias, ada_w, ada_b, norm_pre, norm_post, ffn_w_gate, ffn_w_up, ffn_w_down,
           mix_w_in, mix_b_in, mix_w_out, attn_sinks, group_gain):
    n_batch, seq, d = x.shape
    depth = ada_w.shape[0]
    n_heads = d // HEAD_DIM
    swa_q_heads = n_heads // 2
    swa_kv_heads = max(1, swa_q_heads // 4)
    moba_heads = n_heads - swa_q_heads
    swa_w = swa_q_heads * HEAD_DIM
    kv_w = swa_kv_heads * HEAD_DIM
    moba_w = moba_heads * HEAD_DIM
    q_col = (swa_w + 2 * kv_w) // LANES
    k_col = q_col + moba_w // LANES
    v_col = k_col + moba_w // LANES

    mod = _modulation(c, ada_w, ada_b).reshape(depth, n_batch, N_MOD, d)
    bias_swa, bias_own, bias_prev, bias_const = _bias_tables(rel_bias, swa_q_heads)
    sinks = jnp.broadcast_to(attn_sinks[:, :, None], attn_sinks.shape + (LANES,))

    wg = ffn_w_gate.astype(BF16)
    wu = ffn_w_up.astype(BF16)
    wd = ffn_w_down.astype(BF16)
    w_in = mix_w_in.astype(BF16)
    w_out = mix_w_out.astype(BF16)
    b_in = mix_b_in[:, None, :]
    gpre = norm_pre[:, :, None, None, :]
    gpost = norm_post[:, :, None, None, :]
    gg = group_gain[:, None, None, :]

    xt = x.reshape(n_batch * seq, d)
    for l in range(depth):
        xt = _ffn(xt, mod[l], gpre[l, 0], gpost[l, 0], wg, wu, wd, l, 0, seq)
        qkv = _inproj(xt, mod[l], gpre[l, 1], w_in, b_in, l, seq)
        ya = _swa(qkv, bias_swa, sinks[l], n_batch, seq, swa_q_heads, swa_kv_heads)
        yb = _moba(qkv, bias_own, bias_prev, bias_const, n_batch, seq, q_col, k_col, v_col,
                   moba_heads)
        xt = _outproj(xt, ya, yb, mod[l], gg[l], gpost[l, 1], w_out, l, seq)
        xt = _ffn(xt, mod[l], gpre[l, 2], gpost[l, 2], wg, wu, wd, l, 1, seq)
    return xt.reshape(n_batch, seq, d)
```

```python
import functools
import math

import numpy as np
import jax
import jax.numpy as jnp
from jax import lax
from jax.experimental import pallas as pl
from jax.experimental.pallas import tpu as pltpu

HEAD_DIM = 64
LANES = 128
SWA_BLOCK = 128
SWA_WINDOW = 128
MOBA_BLOCK = 256
MOBA_TOPK = 3
FAR_GROUP = 4
T5_BUCKETS = 32
T5_MAX_DIST = 128
N_MOD = 9
EPS = 1e-6
NEG = -1e30
VMEM_LIMIT = 56 * 1024 * 1024

F32 = jnp.float32
BF16 = jnp.bfloat16


def _cparams(n_axes):
    return pltpu.CompilerParams(
        dimension_semantics=("arbitrary",) * n_axes,
        vmem_limit_bytes=VMEM_LIMIT,
    )


def _rms(x, g):
    return x * lax.rsqrt(jnp.mean(x * x, axis=-1, keepdims=True) + EPS) * g


def _dot(a, b):
    return jnp.dot(a, b, preferred_element_type=F32)


def _dot_nt(a, b):
    return lax.dot_general(a, b, (((1,), (1,)), ((), ())), preferred_element_type=F32)


def _mod_kernel(ct_ref, w_ref, b_ref, o_ref, *, n_batch, row_chunk):
    ct = ct_ref[...]
    ca = ct * jax.nn.sigmoid(ct)
    d = ct.shape[0]
    for b in range(n_batch):
        acc = b_ref[0]
        for r in range(0, d, row_chunk):
            w = w_ref[0, r:r + row_chunk, :]
            acc = acc + jnp.sum(w * ca[r:r + row_chunk, b:b + 1], axis=0, keepdims=True)
        o_ref[0, b:b + 1, :] = acc


def _modulation(c, ada_w, ada_b, *, tn=1152, row_chunk=128):
    depth, d, n = ada_w.shape
    n_batch = c.shape[0]
    return pl.pallas_call(
        functools.partial(_mod_kernel, n_batch=n_batch, row_chunk=row_chunk),
        grid=(depth, n // tn),
        in_specs=[
            pl.BlockSpec((d, n_batch), lambda l, j: (0, 0)),
            pl.BlockSpec((1, d, tn), lambda l, j: (l, 0, j)),
            pl.BlockSpec((1, 1, tn), lambda l, j: (l, 0, j)),
        ],
        out_specs=pl.BlockSpec((1, n_batch, tn), lambda l, j: (l, 0, j)),
        out_shape=jax.ShapeDtypeStruct((depth, n_batch, n), F32),
        compiler_params=_cparams(2),
        name="adaln_modulation",
    )(c.T, ada_w, ada_b.reshape(depth, 1, n))


def _ffn_kernel(x_ref, mod_ref, gpre_ref, gpost_ref, wg_ref, wu_ref, wd_ref, o_ref,
                *, mod_base, ff_chunk):
    x = x_ref[...]
    mod = mod_ref[0]
    shift = mod[mod_base:mod_base + 1]
    scale = mod[mod_base + 1:mod_base + 2]
    gate = mod[mod_base + 2:mod_base + 3]
    h = (_rms(x, gpre_ref[0]) * (1.0 + scale) + shift).astype(BF16)
    d_ff = wg_ref.shape[-1]
    acc = jnp.zeros(x.shape, F32)
    for c0 in range(0, d_ff, ff_chunk):
        g = _dot(h, wg_ref[0, 0, :, c0:c0 + ff_chunk])
        u = _dot(h, wu_ref[0, 0, :, c0:c0 + ff_chunk])
        a = (g * jax.nn.sigmoid(g) * u).astype(BF16)
        acc = acc + _dot(a, wd_ref[0, 0, c0:c0 + ff_chunk, :])
    o_ref[...] = x + 0.5 * gate * _rms(acc, gpost_ref[0])


def _ffn(x, mod_l, gpre, gpost, wg, wu, wd, layer, which, seq, *, tm=512, ff_chunk=1408):
    t, d = x.shape
    d_ff = wg.shape[-1]
    tiles_per_batch = seq // tm
    const = pl.Buffered(1)
    return pl.pallas_call(
        functools.partial(_ffn_kernel, mod_base=0 if which == 0 else 6, ff_chunk=ff_chunk),
        grid=(t // tm,),
        in_specs=[
            pl.BlockSpec((tm, d), lambda i: (i, 0)),
            pl.BlockSpec((1, N_MOD, d), lambda i: (i // tiles_per_batch, 0, 0)),
            pl.BlockSpec((1, 1, d), lambda i: (0, 0, 0)),
            pl.BlockSpec((1, 1, d), lambda i: (0, 0, 0)),
            pl.BlockSpec((1, 1, d, d_ff), lambda i: (layer, which, 0, 0), pipeline_mode=const),
            pl.BlockSpec((1, 1, d, d_ff), lambda i: (layer, which, 0, 0), pipeline_mode=const),
            pl.BlockSpec((1, 1, d_ff, d), lambda i: (layer, which, 0, 0), pipeline_mode=const),
        ],
        out_specs=pl.BlockSpec((tm, d), lambda i: (i, 0)),
        out_shape=jax.ShapeDtypeStruct((t, d), F32),
        compiler_params=_cparams(1),
        name="swiglu_half_step",
    )(x, mod_l, gpre, gpost, wg, wu, wd)


def _inproj_kernel(x_ref, mod_ref, gpre_ref, w_ref, b_ref, o_ref):
    x = x_ref[...]
    mod = mod_ref[0]
    h = (_rms(x, gpre_ref[0]) * (1.0 + mod[4:5]) + mod[3:4]).astype(BF16)
    o_ref[...] = (_dot(h, w_ref[0]) + b_ref[0]).astype(o_ref.dtype)


def _inproj(x, mod_l, gpre, w_in, b_in, layer, seq, *, tm=512):
    t, d = x.shape
    n = w_in.shape[-1]
    tiles_per_batch = seq // tm
    return pl.pallas_call(
        _inproj_kernel,
        grid=(t // tm,),
        in_specs=[
            pl.BlockSpec((tm, d), lambda i: (i, 0)),
            pl.BlockSpec((1, N_MOD, d), lambda i: (i // tiles_per_batch, 0, 0)),
            pl.BlockSpec((1, 1, d), lambda i: (0, 0, 0)),
            pl.BlockSpec((1, d, n), lambda i: (layer, 0, 0), pipeline_mode=pl.Buffered(1)),
            pl.BlockSpec((1, 1, n), lambda i: (layer, 0, 0)),
        ],
        out_specs=pl.BlockSpec((tm, n), lambda i: (i, 0)),
        out_shape=jax.ShapeDtypeStruct((t, n), BF16),
        compiler_params=_cparams(1),
        name="qkv_projection",
    )(x, mod_l, gpre, w_in, b_in)


def _roll_heads(x):
    return pltpu.roll(x.astype(F32), HEAD_DIM, axis=1).astype(x.dtype)


def _swa_kernel(q_ref, kp_ref, km_ref, vp_ref, vm_ref, bias_ref, sink_ref, o_ref,
                *, n_q_heads, n_kv_heads):
    tq = q_ref.shape[0]
    group = n_q_heads // n_kv_heads
    i = pl.program_id(1)
    k_all = jnp.concatenate([kp_ref[...], km_ref[...]], axis=0)
    v_all = jnp.concatenate([vp_ref[...], vm_ref[...]], axis=0)
    k_rolled = _roll_heads(k_all)
    v_rolled = _roll_heads(v_all)
    lane = lax.broadcasted_iota(jnp.int32, (SWA_BLOCK, LANES), 1)
    lo = lane < HEAD_DIM
    col = lax.broadcasted_iota(jnp.int32, (SWA_BLOCK, 2 * SWA_BLOCK), 1)
    scale = HEAD_DIM ** -0.5
    for blk in range(tq // SWA_BLOCK):
        r0 = blk * SWA_BLOCK
        kw = (k_all[r0:r0 + 2 * SWA_BLOCK], k_rolled[r0:r0 + 2 * SWA_BLOCK])
        vw = (v_all[r0:r0 + 2 * SWA_BLOCK], v_rolled[r0:r0 + 2 * SWA_BLOCK])
        for pair in range(n_q_heads // 2):
            qp = (q_ref[r0:r0 + SWA_BLOCK, pair * LANES:(pair + 1) * LANES].astype(F32)
                  * scale).astype(BF16)
            outs = []
            for odd in range(2):
                h = 2 * pair + odd
                kv = h // group
                use_roll = (kv + odd) % 2
                qm = jnp.where(lo if odd == 0 else jnp.logical_not(lo), qp, jnp.zeros_like(qp))
                s = _dot_nt(qm, kw[use_roll])
                b = bias_ref[h]
                s = jnp.where(b > 0.5 * NEG, s + b, NEG)
                if blk == 0:
                    s = jnp.where(jnp.logical_and(col < SWA_BLOCK, i == 0), NEG, s)
                sink = sink_ref[h:h + 1, 0:1]
                m = jnp.maximum(jnp.max(s, axis=1, keepdims=True), sink)
                p = jnp.exp(s - m)
                denom = jnp.sum(p, axis=1, keepdims=True) + jnp.exp(sink - m)
                outs.append(_dot(p.astype(BF16), vw[use_roll]) / denom)
            o_ref[r0:r0 + SWA_BLOCK, pair * LANES:(pair + 1) * LANES] = jnp.where(
                lo, outs[0], outs[1])


def _swa(qkv, bias, sinks, n_batch, seq, n_q_heads, n_kv_heads, *, tq=512):
    t = qkv.shape[0]
    nq = seq // tq
    qw = n_q_heads * HEAD_DIM
    k_col = qw // LANES
    v_col = k_col + (n_kv_heads * HEAD_DIM) // LANES
    per = tq // SWA_BLOCK

    def prev_map(col):
        return lambda b, i: (jnp.maximum((b * nq + i) * per - 1, 0), col)

    return pl.pallas_call(
        functools.partial(_swa_kernel, n_q_heads=n_q_heads, n_kv_heads=n_kv_heads),
        grid=(n_batch, nq),
        in_specs=[
            pl.BlockSpec((tq, qw), lambda b, i: (b * nq + i, 0)),
            pl.BlockSpec((SWA_BLOCK, LANES), prev_map(k_col)),
            pl.BlockSpec((tq, LANES), lambda b, i: (b * nq + i, k_col)),
            pl.BlockSpec((SWA_BLOCK, LANES), prev_map(v_col)),
            pl.BlockSpec((tq, LANES), lambda b, i: (b * nq + i, v_col)),
            pl.BlockSpec((n_q_heads, SWA_BLOCK, 2 * SWA_BLOCK), lambda b, i: (0, 0, 0)),
            pl.BlockSpec((n_q_heads, LANES), lambda b, i: (0, 0)),
        ],
        out_specs=pl.BlockSpec((tq, qw), lambda b, i: (b * nq + i, 0)),
        out_shape=jax.ShapeDtypeStruct((t, qw), F32),
        compiler_params=_cparams(2),
        name="swa_attention",
    )(qkv, qkv, qkv, qkv, qkv, bias, sinks)


def _moba_kernel(q_ref, k_ref, v_ref, bown_ref, bprev_ref, bconst_ref, o_ref, kmean_ref, vt_ref,
                 *, n_blocks):
    L = MOBA_BLOCK
    c = pl.program_id(2)

    @pl.when(c == 0)
    def _():
        kmean_ref[...] = jnp.zeros_like(kmean_ref)
        for jb in range(n_blocks):
            kb = k_ref[jb * L:(jb + 1) * L, :].astype(F32)
            kmean_ref[jb:jb + 1, :] = jnp.sum(kb, axis=0, keepdims=True) * (1.0 / L)
            vt_ref[jb] = v_ref[jb * L:(jb + 1) * L, :].astype(F32).T.astype(BF16)

    scale = HEAD_DIM ** -0.5
    qt = (q_ref[...].astype(F32) * scale).T
    lo = lax.broadcasted_iota(jnp.int32, (LANES, L), 0) < HEAD_DIM
    zero = jnp.zeros_like(qt)
    q2t = jnp.concatenate([jnp.where(lo, qt, zero), jnp.where(lo, zero, qt)],
                          axis=1).astype(BF16)
    cols = 2 * L

    km = kmean_ref[...]
    km_hi = km.astype(BF16)
    km_lo = (km - km_hi.astype(F32)).astype(BF16)
    gate = _dot(km_hi, q2t) + _dot(km_lo, q2t)
    rowf = lax.broadcasted_iota(jnp.int32, gate.shape, 0).astype(F32)
    g = jnp.where(rowf < c.astype(F32), gate, NEG)
    sels = []
    for t in range(MOBA_TOPK):
        gmax = jnp.max(g, axis=0, keepdims=True)
        idx = jnp.min(jnp.where(g == gmax, rowf, float(n_blocks)), axis=0, keepdims=True)
        sels.append(jnp.where(t < c, idx, -1.0))
        g = jnp.where(rowf == idx, -jnp.inf, g)

    def selected(jf):
        hit = sels[0] == jf
        for s_t in sels[1:]:
            hit = jnp.logical_or(hit, s_t == jf)
        return hit

    def scores(j, n=1):
        return _dot(k_ref[pl.ds(pl.multiple_of(j * L, L), n * L), :], q2t)

    def attend(sts, js, carry):
        mx = jnp.max(sts[0], axis=0, keepdims=True)
        for st in sts[1:]:
            mx = jnp.maximum(mx, jnp.max(st, axis=0, keepdims=True))
        if carry is None:
            m_new = mx
        else:
            m, l, acc = carry
            m_new = jnp.maximum(m, mx)
            alpha = jnp.exp(m - m_new)
            l, acc = alpha * l, alpha * acc
        for st, j in zip(sts, js):
            p = jnp.exp(st - m_new)
            p_sum = jnp.sum(p, axis=0, keepdims=True)
            pv = _dot(vt_ref[j], p.astype(BF16))
            if carry is None:
                l, acc, carry = p_sum, pv, ()
            else:
                l, acc = l + p_sum, acc + pv
        return m_new, l, acc

    b_own = bown_ref[0]
    st_own = jnp.where(b_own > 0.5 * NEG, scores(c) + b_own, NEG)
    jp = jnp.maximum(c - 1, 0)
    sel_prev = jnp.logical_and(selected((c - 1).astype(F32)), c > 0)
    st_prev = jnp.where(sel_prev, scores(jp) + bprev_ref[0], NEG)
    m, l, acc = attend([st_own, st_prev], [c, jp], None)

    def far_group(i, carry):
        j0 = i * FAR_GROUP
        st = scores(j0, FAR_GROUP)
        sts = [jnp.where(selected((j0 + g).astype(F32)), st[g * L:(g + 1) * L], NEG)
               for g in range(FAR_GROUP)]
        return attend(sts, [j0 + g for g in range(FAR_GROUP)], carry)

    def far_block(j, carry):
        st = jnp.where(selected(j.astype(F32)), scores(j), NEG)
        return attend([st], [j], carry)

    n_far = jnp.maximum(c - 1, 0)
    n_groups = n_far // FAR_GROUP
    carry = lax.fori_loop(0, n_groups, far_group, (m - bconst_ref[0], l, acc))
    _, l, acc = lax.fori_loop(n_groups * FAR_GROUP, n_far, far_block, carry)
    out_t = acc * (1.0 / l)
    o_ref[...] = jnp.where(lo, out_t[:, :L], out_t[:, L:]).T


def _moba(qkv, bown_t, bprev_t, bconst, n_batch, seq, q_col, k_col, v_col, n_heads):
    t = qkv.shape[0]
    L = MOBA_BLOCK
    nb = seq // L
    nbp = -(-nb // 8) * 8
    n_pairs = n_heads // 2
    return pl.pallas_call(
        functools.partial(_moba_kernel, n_blocks=nb),
        grid=(n_batch, n_pairs, nb),
        in_specs=[
            pl.BlockSpec((L, LANES), lambda b, p, c: (b * nb + c, q_col + p)),
            pl.BlockSpec((seq, LANES), lambda b, p, c: (b, k_col + p)),
            pl.BlockSpec((seq, LANES), lambda b, p, c: (b, v_col + p)),
            pl.BlockSpec((1, L, 2 * L), lambda b, p, c: (p, 0, 0)),
            pl.BlockSpec((1, L, 2 * L), lambda b, p, c: (p, 0, 0)),
            pl.BlockSpec((1, 1, 2 * L), lambda b, p, c: (p, 0, 0)),
        ],
        out_specs=pl.BlockSpec((L, LANES), lambda b, p, c: (b * nb + c, p)),
        out_shape=jax.ShapeDtypeStruct((t, n_heads * HEAD_DIM), F32),
        scratch_shapes=[pltpu.VMEM((nbp, LANES), F32), pltpu.VMEM((nb, LANES, L), BF16)],
        compiler_params=_cparams(3),
        name="moba_attention",
    )(qkv, qkv, qkv, bown_t, bprev_t, bconst)


def _outproj_kernel(x_ref, ya_ref, yb_ref, mod_ref, gg_ref, gpost_ref, w_ref, o_ref):
    wa = ya_ref.shape[1]
    gg = gg_ref[0]
    ya = _rms(ya_ref[...], gg[:, :wa]).astype(BF16)
    yb = _rms(yb_ref[...], gg[:, wa:]).astype(BF16)
    y = _dot(ya, w_ref[0, :wa, :]) + _dot(yb, w_ref[0, wa:, :])
    gate = mod_ref[0][5:6]
    o_ref[...] = x_ref[...] + gate * _rms(y, gpost_ref[0])


def _outproj(x, ya, yb, mod_l, gg, gpost, w_out, layer, seq, *, tm=512):
    t, d = x.shape
    wa, wb = ya.shape[1], yb.shape[1]
    tiles_per_batch = seq // tm
    return pl.pallas_call(
        _outproj_kernel,
        grid=(t // tm,),
        in_specs=[
            pl.BlockSpec((tm, d), lambda i: (i, 0)),
            pl.BlockSpec((tm, wa), lambda i: (i, 0)),
            pl.BlockSpec((tm, wb), lambda i: (i, 0)),
            pl.BlockSpec((1, N_MOD, d), lambda i: (i // tiles_per_batch, 0, 0)),
            pl.BlockSpec((1, 1, wa + wb), lambda i: (0, 0, 0)),
            pl.BlockSpec((1, 1, d), lambda i: (0, 0, 0)),
            pl.BlockSpec((1, wa + wb, d), lambda i: (layer, 0, 0), pipeline_mode=pl.Buffered(1)),
        ],
        out_specs=pl.BlockSpec((tm, d), lambda i: (i, 0)),
        out_shape=jax.ShapeDtypeStruct((t, d), F32),
        compiler_params=_cparams(1),
        name="mixer_out_projection",
    )(x, ya, yb, mod_l, gg, gpost, w_out)


def _t5_bucket_np(dist):
    n = np.maximum(dist, 0)
    max_exact = T5_BUCKETS // 2
    nf = np.maximum(n, 1).astype(np.float32)
    large = max_exact + (np.log(nf / np.float32(max_exact)) / np.float32(math.log(T5_MAX_DIST / max_exact))
                         * np.float32(T5_BUCKETS - max_exact)).astype(np.int32)
    large = np.minimum(large, T5_BUCKETS - 1)
    return np.where(n < max_exact, n, large).astype(np.int32)


def _bias_kernel(tab_ref, *refs, head_base, heads_per_step):
    n = len(refs) // 2
    step = pl.program_id(0)
    for idx_ref, o_ref in zip(refs[:n], refs[n:]):
        idx = idx_ref[...]
        width = idx.shape[1]
        for hh in range(heads_per_step):
            head = head_base + step * heads_per_step + hh
            out = jnp.full(idx.shape, NEG, F32)
            for bucket in range(T5_BUCKETS):
                out = jnp.where(idx == bucket, tab_ref[bucket, head], out)
            o_ref[0, :, hh * width:(hh + 1) * width] = out


def _bias_expand(rel_bias, buckets, head_base, n_steps, heads_per_step):
    shapes = [b.shape for b in buckets]
    return pl.pallas_call(
        functools.partial(_bias_kernel, head_base=head_base, heads_per_step=heads_per_step),
        grid=(n_steps,),
        in_specs=[pl.BlockSpec(memory_space=pltpu.SMEM)]
        + [pl.BlockSpec(s, lambda h: (0, 0)) for s in shapes],
        out_specs=[pl.BlockSpec((1, s[0], heads_per_step * s[1]), lambda h: (h, 0, 0))
                   for s in shapes],
        out_shape=[jax.ShapeDtypeStruct((n_steps, s[0], heads_per_step * s[1]), F32)
                   for s in shapes],
        compiler_params=_cparams(1),
        name="t5_bias_tables",
    )(rel_bias, *[jnp.asarray(b) for b in buckets])


def _bias_tables(rel_bias, n_swa_heads):
    n_heads = rel_bias.shape[1]
    qi = np.arange(SWA_BLOCK)[:, None]
    kj = np.arange(2 * SWA_BLOCK)[None, :]
    dist = qi + SWA_BLOCK - kj
    swa_idx = np.where((dist >= 0) & (dist < SWA_WINDOW), _t5_bucket_np(dist), -1).astype(np.int32)
    (swa,) = _bias_expand(rel_bias, [swa_idx], 0, n_swa_heads, 1)

    kk = np.arange(MOBA_BLOCK)[:, None]
    qq = np.arange(MOBA_BLOCK)[None, :]
    d_own = qq - kk
    own_idx = np.where(d_own >= 0, _t5_bucket_np(d_own), -1).astype(np.int32)
    prev_idx = _t5_bucket_np(d_own + MOBA_BLOCK)
    n_pairs = (n_heads - n_swa_heads) // 2
    own_t, prev_t = _bias_expand(rel_bias, [own_idx, prev_idx], n_swa_heads, n_pairs, 2)
    far = int(_t5_bucket_np(np.array(MOBA_BLOCK + 1)))
    assert far == int(_t5_bucket_np(np.array(1 << 30)))
    const = jnp.repeat(rel_bias[far, n_swa_heads:], MOBA_BLOCK).reshape(n_pairs, 1, 2 * MOBA_BLOCK)
    return swa, own_t, prev_t, const


def kernel(x, c, rel_bias, ada_w, ada_b, norm_pre, norm_post, ffn_w_gate, ffn_w_up, ffn_w_down,
           mix_w_in, mix_b_in, mix_w_out, attn_sinks, group_gain):
    n_batch, seq, d = x.shape
    depth = ada_w.shape[0]
    n_heads = d // HEAD_DIM
    swa_q_heads = n_heads // 2
    swa_kv_heads = max(1, swa_q_heads // 4)
    moba_heads = n_heads - swa_q_heads
    swa_w = swa_q_heads * HEAD_DIM
    kv_w = swa_kv_heads * HEAD_DIM
    moba_w = moba_heads * HEAD_DIM
    q_col = (swa_w + 2 * kv_w) // LANES
    k_col = q_col + moba_w // LANES
    v_col = k_col + moba_w // LANES

    mod = _modulation(c, ada_w, ada_b).reshape(depth, n_batch, N_MOD, d)
    bias_swa, bias_own_t, bias_prev_t, bias_const = _bias_tables(rel_bias, swa_q_heads)
    sinks = jnp.broadcast_to(attn_sinks[:, :, None], attn_sinks.shape + (LANES,))

    wg = ffn_w_gate.astype(BF16)
    wu = ffn_w_up.astype(BF16)
    wd = ffn_w_down.astype(BF16)
    w_in = mix_w_in.astype(BF16)
    w_out = mix_w_out.astype(BF16)
    b_in = mix_b_in[:, None, :]
    gpre = norm_pre[:, :, None, None, :]
    gpost = norm_post[:, :, None, None, :]
    gg = group_gain[:, None, None, :]

    xt = x.reshape(n_batch * seq, d)
    for l in range(depth):
        xt = _ffn(xt, mod[l], gpre[l, 0], gpost[l, 0], wg, wu, wd, l, 0, seq)
        qkv = _inproj(xt, mod[l], gpre[l, 1], w_in, b_in, l, seq)
        ya = _swa(qkv, bias_swa, sinks[l], n_batch, seq, swa_q_heads, swa_kv_heads)
        yb = _moba(qkv, bias_own_t, bias_prev_t, bias_const, n_batch, seq, q_col, k_col, v_col,
                   moba_heads)
        xt = _outproj(xt, ya, yb, mod[l], gg[l], gpost[l, 1], w_out, l, seq)
        xt = _ffn(xt, mod[l], gpre[l, 2], gpost[l, 2], wg, wu, wd, l, 1, seq)
    return xt.reshape(n_batch, seq, d)
```

```python
import functools
import math

import numpy as np
import jax
import jax.numpy as jnp
from jax import lax
from jax.experimental import pallas as pl
from jax.experimental.pallas import tpu as pltpu

HEAD_DIM = 64
LANES = 128
SWA_BLOCK = 128
SWA_WINDOW = 128
MOBA_BLOCK = 256
MOBA_TOPK = 3
FAR_GROUPS = (8, 2, 1)
ONES_ROWS = 16
T5_BUCKETS = 32
T5_MAX_DIST = 128
N_MOD = 9
EPS = 1e-6
NEG = -1e30
VMEM_LIMIT = 56 * 1024 * 1024

F32 = jnp.float32
BF16 = jnp.bfloat16


def _cparams(n_axes):
    return pltpu.CompilerParams(
        dimension_semantics=("arbitrary",) * n_axes,
        vmem_limit_bytes=VMEM_LIMIT,
    )


def _rms(x, g):
    return x * lax.rsqrt(jnp.mean(x * x, axis=-1, keepdims=True) + EPS) * g


def _dot(a, b):
    return jnp.dot(a, b, preferred_element_type=F32)


def _dot_nt(a, b):
    return lax.dot_general(a, b, (((1,), (1,)), ((), ())), preferred_element_type=F32)


def _mod_kernel(ct_ref, w_ref, b_ref, o_ref, *, n_batch, row_chunk):
    ct = ct_ref[...]
    ca = ct * jax.nn.sigmoid(ct)
    d = ct.shape[0]
    for b in range(n_batch):
        acc = b_ref[0]
        for r in range(0, d, row_chunk):
            w = w_ref[0, r:r + row_chunk, :]
            acc = acc + jnp.sum(w * ca[r:r + row_chunk, b:b + 1], axis=0, keepdims=True)
        o_ref[0, b:b + 1, :] = acc


def _modulation(c, ada_w, ada_b, *, tn=1152, row_chunk=128):
    depth, d, n = ada_w.shape
    n_batch = c.shape[0]
    return pl.pallas_call(
        functools.partial(_mod_kernel, n_batch=n_batch, row_chunk=row_chunk),
        grid=(depth, n // tn),
        in_specs=[
            pl.BlockSpec((d, n_batch), lambda l, j: (0, 0)),
            pl.BlockSpec((1, d, tn), lambda l, j: (l, 0, j)),
            pl.BlockSpec((1, 1, tn), lambda l, j: (l, 0, j)),
        ],
        out_specs=pl.BlockSpec((1, n_batch, tn), lambda l, j: (l, 0, j)),
        out_shape=jax.ShapeDtypeStruct((depth, n_batch, n), F32),
        compiler_params=_cparams(2),
        name="adaln_modulation",
    )(c.T, ada_w, ada_b.reshape(depth, 1, n))


def _ffn_kernel(x_ref, mod_ref, gpre_ref, gpost_ref, wg_ref, wu_ref, wd_ref, o_ref,
                *, mod_base, ff_chunk):
    x = x_ref[...]
    mod = mod_ref[0]
    shift = mod[mod_base:mod_base + 1]
    scale = mod[mod_base + 1:mod_base + 2]
    gate = mod[mod_base + 2:mod_base + 3]
    h = (_rms(x, gpre_ref[0]) * (1.0 + scale) + shift).astype(BF16)
    d_ff = wg_ref.shape[-1]
    acc = jnp.zeros(x.shape, F32)
    for c0 in range(0, d_ff, ff_chunk):
        g = _dot(h, wg_ref[0, 0, :, c0:c0 + ff_chunk])
        u = _dot(h, wu_ref[0, 0, :, c0:c0 + ff_chunk])
        a = (g * jax.nn.sigmoid(g) * u).astype(BF16)
        acc = acc + _dot(a, wd_ref[0, 0, c0:c0 + ff_chunk, :])
    o_ref[...] = x + 0.5 * gate * _rms(acc, gpost_ref[0])


def _ffn(x, mod_l, gpre, gpost, wg, wu, wd, layer, which, seq, *, tm=512, ff_chunk=2816):
    t, d = x.shape
    d_ff = wg.shape[-1]
    tiles_per_batch = seq // tm
    const = pl.Buffered(1)
    return pl.pallas_call(
        functools.partial(_ffn_kernel, mod_base=0 if which == 0 else 6, ff_chunk=ff_chunk),
        grid=(t // tm,),
        in_specs=[
            pl.BlockSpec((tm, d), lambda i: (i, 0)),
            pl.BlockSpec((1, N_MOD, d), lambda i: (i // tiles_per_batch, 0, 0)),
            pl.BlockSpec((1, 1, d), lambda i: (0, 0, 0)),
            pl.BlockSpec((1, 1, d), lambda i: (0, 0, 0)),
            pl.BlockSpec((1, 1, d, d_ff), lambda i: (layer, which, 0, 0), pipeline_mode=const),
            pl.BlockSpec((1, 1, d, d_ff), lambda i: (layer, which, 0, 0), pipeline_mode=const),
            pl.BlockSpec((1, 1, d_ff, d), lambda i: (layer, which, 0, 0), pipeline_mode=const),
        ],
        out_specs=pl.BlockSpec((tm, d), lambda i: (i, 0)),
        out_shape=jax.ShapeDtypeStruct((t, d), F32),
        compiler_params=_cparams(1),
        name="swiglu_half_step",
    )(x, mod_l, gpre, gpost, wg, wu, wd)


def _inproj_kernel(x_ref, mod_ref, gpre_ref, w_ref, b_ref, o_ref):
    x = x_ref[...]
    mod = mod_ref[0]
    h = (_rms(x, gpre_ref[0]) * (1.0 + mod[4:5]) + mod[3:4]).astype(BF16)
    o_ref[...] = (_dot(h, w_ref[0]) + b_ref[0]).astype(o_ref.dtype)


def _inproj(x, mod_l, gpre, w_in, b_in, layer, seq, *, tm=512):
    t, d = x.shape
    n = w_in.shape[-1]
    tiles_per_batch = seq // tm
    return pl.pallas_call(
        _inproj_kernel,
        grid=(t // tm,),
        in_specs=[
            pl.BlockSpec((tm, d), lambda i: (i, 0)),
            pl.BlockSpec((1, N_MOD, d), lambda i: (i // tiles_per_batch, 0, 0)),
            pl.BlockSpec((1, 1, d), lambda i: (0, 0, 0)),
            pl.BlockSpec((1, d, n), lambda i: (layer, 0, 0), pipeline_mode=pl.Buffered(1)),
            pl.BlockSpec((1, 1, n), lambda i: (layer, 0, 0)),
        ],
        out_specs=pl.BlockSpec((tm, n), lambda i: (i, 0)),
        out_shape=jax.ShapeDtypeStruct((t, n), BF16),
        compiler_params=_cparams(1),
        name="qkv_projection",
    )(x, mod_l, gpre, w_in, b_in)


def _roll_heads(x):
    return pltpu.roll(x.astype(F32), HEAD_DIM, axis=1).astype(x.dtype)


def _swa_kernel(q_ref, kp_ref, km_ref, vp_ref, vm_ref, bias_ref, sink_ref, o_ref,
                *, n_q_heads, n_kv_heads):
    tq = q_ref.shape[0]
    group = n_q_heads // n_kv_heads
    i = pl.program_id(1)
    k_all = jnp.concatenate([kp_ref[...], km_ref[...]], axis=0)
    v_all = jnp.concatenate([vp_ref[...], vm_ref[...]], axis=0)
    k_rolled = _roll_heads(k_all)
    v_rolled = _roll_heads(v_all)
    lane = lax.broadcasted_iota(jnp.int32, (SWA_BLOCK, LANES), 1)
    lo = lane < HEAD_DIM
    col = lax.broadcasted_iota(jnp.int32, (SWA_BLOCK, 2 * SWA_BLOCK), 1)
    scale = HEAD_DIM ** -0.5
    for blk in range(tq // SWA_BLOCK):
        r0 = blk * SWA_BLOCK
        kw = (k_all[r0:r0 + 2 * SWA_BLOCK], k_rolled[r0:r0 + 2 * SWA_BLOCK])
        vw = (v_all[r0:r0 + 2 * SWA_BLOCK], v_rolled[r0:r0 + 2 * SWA_BLOCK])
        for pair in range(n_q_heads // 2):
            qp = (q_ref[r0:r0 + SWA_BLOCK, pair * LANES:(pair + 1) * LANES].astype(F32)
                  * scale).astype(BF16)
            outs = []
            for odd in range(2):
                h = 2 * pair + odd
                kv = h // group
                use_roll = (kv + odd) % 2
                qm = jnp.where(lo if odd == 0 else jnp.logical_not(lo), qp, jnp.zeros_like(qp))
                s = _dot_nt(qm, kw[use_roll])
                b = bias_ref[h]
                s = jnp.where(b > 0.5 * NEG, s + b, NEG)
                if blk == 0:
                    s = jnp.where(jnp.logical_and(col < SWA_BLOCK, i == 0), NEG, s)
                sink = sink_ref[h:h + 1, 0:1]
                m = jnp.maximum(jnp.max(s, axis=1, keepdims=True), sink)
                p = jnp.exp(s - m)
                denom = jnp.sum(p, axis=1, keepdims=True) + jnp.exp(sink - m)
                outs.append(_dot(p.astype(BF16), vw[use_roll]) / denom)
            o_ref[r0:r0 + SWA_BLOCK, pair * LANES:(pair + 1) * LANES] = jnp.where(
                lo, outs[0], outs[1])


def _swa(qkv, bias, sinks, n_batch, seq, n_q_heads, n_kv_heads, *, tq=512):
    t = qkv.shape[0]
    nq = seq // tq
    qw = n_q_heads * HEAD_DIM
    k_col = qw // LANES
    v_col = k_col + (n_kv_heads * HEAD_DIM) // LANES
    per = tq // SWA_BLOCK

    def prev_map(col):
        return lambda b, i: (jnp.maximum((b * nq + i) * per - 1, 0), col)

    return pl.pallas_call(
        functools.partial(_swa_kernel, n_q_heads=n_q_heads, n_kv_heads=n_kv_heads),
        grid=(n_batch, nq),
        in_specs=[
            pl.BlockSpec((tq, qw), lambda b, i: (b * nq + i, 0)),
            pl.BlockSpec((SWA_BLOCK, LANES), prev_map(k_col)),
            pl.BlockSpec((tq, LANES), lambda b, i: (b * nq + i, k_col)),
            pl.BlockSpec((SWA_BLOCK, LANES), prev_map(v_col)),
            pl.BlockSpec((tq, LANES), lambda b, i: (b * nq + i, v_col)),
            pl.BlockSpec((n_q_heads, SWA_BLOCK, 2 * SWA_BLOCK), lambda b, i: (0, 0, 0)),
            pl.BlockSpec((n_q_heads, LANES), lambda b, i: (0, 0)),
        ],
        out_specs=pl.BlockSpec((tq, qw), lambda b, i: (b * nq + i, 0)),
        out_shape=jax.ShapeDtypeStruct((t, qw), F32),
        compiler_params=_cparams(2),
        name="swa_attention",
    )(qkv, qkv, qkv, qkv, qkv, bias, sinks)


def _moba_kernel(q_ref, k_ref, v_ref, bown_ref, bprev_ref, bconst_ref, o_ref, kmean_ref, vt_ref,
                 st_ref, *, n_blocks):
    L = MOBA_BLOCK
    c = pl.program_id(2)

    @pl.when(c == 0)
    def _():
        kmean_ref[...] = jnp.zeros_like(kmean_ref)
        for jb in range(n_blocks):
            kb = k_ref[jb * L:(jb + 1) * L, :].astype(F32)
            kmean_ref[jb:jb + 1, :] = jnp.sum(kb, axis=0, keepdims=True) * (1.0 / L)
            vt_ref[jb, :LANES, :] = v_ref[jb * L:(jb + 1) * L, :].astype(F32).T.astype(BF16)
            vt_ref[jb, LANES:, :] = jnp.ones((ONES_ROWS, L), BF16)

    scale = HEAD_DIM ** -0.5
    qt = (q_ref[...].astype(F32) * scale).T
    lo = lax.broadcasted_iota(jnp.int32, (LANES, L), 0) < HEAD_DIM
    zero = jnp.zeros_like(qt)
    q2t = jnp.concatenate([jnp.where(lo, qt, zero), jnp.where(lo, zero, qt)],
                          axis=1).astype(BF16)
    cols = 2 * L

    km = kmean_ref[...]
    km_hi = km.astype(BF16)
    km_lo = (km - km_hi.astype(F32)).astype(BF16)
    gate = _dot(km_hi, q2t) + _dot(km_lo, q2t)
    rowf = lax.broadcasted_iota(jnp.int32, gate.shape, 0).astype(F32)
    g = jnp.where(rowf < c.astype(F32), gate, NEG)
    sels = []
    for t in range(MOBA_TOPK):
        gmax = jnp.max(g, axis=0, keepdims=True)
        idx = jnp.min(jnp.where(g == gmax, rowf, float(n_blocks)), axis=0, keepdims=True)
        sels.append(jnp.where(t < c, idx, -1.0))
        g = jnp.where(rowf == idx, -jnp.inf, g)

    def selected(jf):
        hit = sels[0] == jf
        for s_t in sels[1:]:
            hit = jnp.logical_or(hit, s_t == jf)
        return hit

    def scores(j, n=1):
        return _dot(k_ref[pl.ds(pl.multiple_of(j * L, L), n * L), :], q2t)

    def colmax(st):
        return jnp.max(st, axis=0, keepdims=True)

    b_own = bown_ref[0]
    st_own = jnp.where(b_own > 0.5 * NEG, scores(c) + b_own, NEG)
    jp = jnp.maximum(c - 1, 0)
    sel_prev = jnp.logical_and(selected((c - 1).astype(F32)), c > 0)
    st_prev = jnp.where(sel_prev, scores(jp) + bprev_ref[0], NEG)
    m_near = jnp.maximum(colmax(st_own), colmax(st_prev))

    def far_scores(j0, n, m_far):
        st = scores(j0, n)
        for g in range(n):
            stg = jnp.where(selected((j0 + g).astype(F32)), st[g * L:(g + 1) * L], NEG)
            st_ref[j0 + g] = stg
            m_far = jnp.maximum(m_far, colmax(stg))
        return m_far

    n_far = jnp.maximum(c - 1, 0)

    def far_sweep(fn, carry):
        start = jnp.int32(0)
        for n in FAR_GROUPS:
            count = lax.shift_right_logical(n_far - start, int(math.log2(n)))
            carry = lax.fori_loop(0, count, lambda i, cr, s=start, n=n: fn(s + i * n, n, cr),
                                  carry)
            start = start + count * n
        return carry

    m_far = far_sweep(far_scores, jnp.full((1, cols), NEG, F32))
    bconst = bconst_ref[0]
    m = jnp.maximum(m_near, m_far + bconst)

    def pv(j, p):
        return _dot(vt_ref[j], p.astype(BF16))

    acc = pv(c, jnp.exp(st_own - m)) + pv(jp, jnp.exp(st_prev - m))
    m_off = m - bconst

    def far_pv(j0, n, acc):
        for g in range(n):
            acc = acc + pv(j0 + g, jnp.exp(st_ref[j0 + g] - m_off))
        return acc

    acc = far_sweep(far_pv, acc)
    out_t = acc[:LANES] * (1.0 / acc[LANES:LANES + 1])
    o_ref[...] = jnp.where(lo, out_t[:, :L], out_t[:, L:]).T


def _moba(qkv, bown_t, bprev_t, bconst, n_batch, seq, q_col, k_col, v_col, n_heads):
    t = qkv.shape[0]
    L = MOBA_BLOCK
    nb = seq // L
    nbp = -(-nb // 8) * 8
    n_pairs = n_heads // 2
    return pl.pallas_call(
        functools.partial(_moba_kernel, n_blocks=nb),
        grid=(n_batch, n_pairs, nb),
        in_specs=[
            pl.BlockSpec((L, LANES), lambda b, p, c: (b * nb + c, q_col + p)),
            pl.BlockSpec((seq, LANES), lambda b, p, c: (b, k_col + p)),
            pl.BlockSpec((seq, LANES), lambda b, p, c: (b, v_col + p)),
            pl.BlockSpec((1, L, 2 * L), lambda b, p, c: (p, 0, 0)),
            pl.BlockSpec((1, L, 2 * L), lambda b, p, c: (p, 0, 0)),
            pl.BlockSpec((1, 1, 2 * L), lambda b, p, c: (p, 0, 0)),
        ],
        out_specs=pl.BlockSpec((L, LANES), lambda b, p, c: (b * nb + c, p)),
        out_shape=jax.ShapeDtypeStruct((t, n_heads * HEAD_DIM), F32),
        scratch_shapes=[pltpu.VMEM((nbp, LANES), F32),
                        pltpu.VMEM((nb, LANES + ONES_ROWS, L), BF16),
                        pltpu.VMEM((nb, L, 2 * L), F32)],
        compiler_params=_cparams(3),
        name="moba_attention",
    )(qkv, qkv, qkv, bown_t, bprev_t, bconst)


def _outproj_kernel(x_ref, ya_ref, yb_ref, mod_ref, gg_ref, gpost_ref, w_ref, o_ref):
    wa = ya_ref.shape[1]
    gg = gg_ref[0]
    ya = _rms(ya_ref[...], gg[:, :wa]).astype(BF16)
    yb = _rms(yb_ref[...], gg[:, wa:]).astype(BF16)
    y = _dot(ya, w_ref[0, :wa, :]) + _dot(yb, w_ref[0, wa:, :])
    gate = mod_ref[0][5:6]
    o_ref[...] = x_ref[...] + gate * _rms(y, gpost_ref[0])


def _outproj(x, ya, yb, mod_l, gg, gpost, w_out, layer, seq, *, tm=512):
    t, d = x.shape
    wa, wb = ya.shape[1], yb.shape[1]
    tiles_per_batch = seq // tm
    return pl.pallas_call(
        _outproj_kernel,
        grid=(t // tm,),
        in_specs=[
            pl.BlockSpec((tm, d), lambda i: (i, 0)),
            pl.BlockSpec((tm, wa), lambda i: (i, 0)),
            pl.BlockSpec((tm, wb), lambda i: (i, 0)),
            pl.BlockSpec((1, N_MOD, d), lambda i: (i // tiles_per_batch, 0, 0)),
            pl.BlockSpec((1, 1, wa + wb), lambda i: (0, 0, 0)),
            pl.BlockSpec((1, 1, d), lambda i: (0, 0, 0)),
            pl.BlockSpec((1, wa + wb, d), lambda i: (layer, 0, 0), pipeline_mode=pl.Buffered(1)),
        ],
        out_specs=pl.BlockSpec((tm, d), lambda i: (i, 0)),
        out_shape=jax.ShapeDtypeStruct((t, d), F32),
        compiler_params=_cparams(1),
        name="mixer_out_projection",
    )(x, ya, yb, mod_l, gg, gpost, w_out)


def _t5_bucket_np(dist):
    n = np.maximum(dist, 0)
    max_exact = T5_BUCKETS // 2
    nf = np.maximum(n, 1).astype(np.float32)
    large = max_exact + (np.log(nf / np.float32(max_exact)) / np.float32(math.log(T5_MAX_DIST / max_exact))
                         * np.float32(T5_BUCKETS - max_exact)).astype(np.int32)
    large = np.minimum(large, T5_BUCKETS - 1)
    return np.where(n < max_exact, n, large).astype(np.int32)


def _bias_kernel(tab_ref, *refs, head_base, heads_per_step):
    n = len(refs) // 2
    step = pl.program_id(0)
    for idx_ref, o_ref in zip(refs[:n], refs[n:]):
        idx = idx_ref[...]
        width = idx.shape[1]
        for hh in range(heads_per_step):
            head = head_base + step * heads_per_step + hh
            out = jnp.full(idx.shape, NEG, F32)
            for bucket in range(T5_BUCKETS):
                out = jnp.where(idx == bucket, tab_ref[bucket, head], out)
            o_ref[0, :, hh * width:(hh + 1) * width] = out


def _bias_expand(rel_bias, buckets, head_base, n_steps, heads_per_step):
    shapes = [b.shape for b in buckets]
    return pl.pallas_call(
        functools.partial(_bias_kernel, head_base=head_base, heads_per_step=heads_per_step),
        grid=(n_steps,),
        in_specs=[pl.BlockSpec(memory_space=pltpu.SMEM)]
        + [pl.BlockSpec(s, lambda h: (0, 0)) for s in shapes],
        out_specs=[pl.BlockSpec((1, s[0], heads_per_step * s[1]), lambda h: (h, 0, 0))
                   for s in shapes],
        out_shape=[jax.ShapeDtypeStruct((n_steps, s[0], heads_per_step * s[1]), F32)
                   for s in shapes],
        compiler_params=_cparams(1),
        name="t5_bias_tables",
    )(rel_bias, *[jnp.asarray(b) for b in buckets])


def _bias_tables(rel_bias, n_swa_heads):
    n_heads = rel_bias.shape[1]
    qi = np.arange(SWA_BLOCK)[:, None]
    kj = np.arange(2 * SWA_BLOCK)[None, :]
    dist = qi + SWA_BLOCK - kj
    swa_idx = np.where((dist >= 0) & (dist < SWA_WINDOW), _t5_bucket_np(dist), -1).astype(np.int32)
    (swa,) = _bias_expand(rel_bias, [swa_idx], 0, n_swa_heads, 1)

    kk = np.arange(MOBA_BLOCK)[:, None]
    qq = np.arange(MOBA_BLOCK)[None, :]
    d_own = qq - kk
    own_idx = np.where(d_own >= 0, _t5_bucket_np(d_own), -1).astype(np.int32)
    prev_idx = _t5_bucket_np(d_own + MOBA_BLOCK)
    n_pairs = (n_heads - n_swa_heads) // 2
    own_t, prev_t = _bias_expand(rel_bias, [own_idx, prev_idx], n_swa_heads, n_pairs, 2)
    far = int(_t5_bucket_np(np.array(MOBA_BLOCK + 1)))
    assert far == int(_t5_bucket_np(np.array(1 << 30)))
    const = jnp.repeat(rel_bias[far, n_swa_heads:], MOBA_BLOCK).reshape(n_pairs, 1, 2 * MOBA_BLOCK)
    return swa, own_t, prev_t, const


def kernel(x, c, rel_bias, ada_w, ada_b, norm_pre, norm_post, ffn_w_gate, ffn_w_up, ffn_w_down,
           mix_w_in, mix_b_in, mix_w_out, attn_sinks, group_gain):
    n_batch, seq, d = x.shape
    depth = ada_w.shape[0]
    n_heads = d // HEAD_DIM
    swa_q_heads = n_heads // 2
    swa_kv_heads = max(1, swa_q_heads // 4)
    moba_heads = n_heads - swa_q_heads
    swa_w = swa_q_heads * HEAD_DIM
    kv_w = swa_kv_heads * HEAD_DIM
    moba_w = moba_heads * HEAD_DIM
    q_col = (swa_w + 2 * kv_w) // LANES
    k_col = q_col + moba_w // LANES
    v_col = k_col + moba_w // LANES

    mod = _modulation(c, ada_w, ada_b).reshape(depth, n_batch, N_MOD, d)
    bias_swa, bias_own_t, bias_prev_t, bias_const = _bias_tables(rel_bias, swa_q_heads)
    sinks = jnp.broadcast_to(attn_sinks[:, :, None], attn_sinks.shape + (LANES,))

    wg = ffn_w_gate.astype(BF16)
    wu = ffn_w_up.astype(BF16)
    wd = ffn_w_down.astype(BF16)
    w_in = mix_w_in.astype(BF16)
    w_out = mix_w_out.astype(BF16)
    b_in = mix_b_in[:, None, :]
    gpre = norm_pre[:, :, None, None, :]
    gpost = norm_post[:, :, None, None, :]
    gg = group_gain[:, None, None, :]

    xt = x.reshape(n_batch * seq, d)
    for l in range(depth):
        xt = _ffn(xt, mod[l], gpre[l, 0], gpost[l, 0], wg, wu, wd, l, 0, seq)
        qkv = _inproj(xt, mod[l], gpre[l, 1], w_in, b_in, l, seq)
        ya = _swa(qkv, bias_swa, sinks[l], n_batch, seq, swa_q_heads, swa_kv_heads)
        yb = _moba(qkv, bias_own_t, bias_prev_t, bias_const, n_batch, seq, q_col, k_col, v_col,
                   moba_heads)
        xt = _outproj(xt, ya, yb, mod[l], gg[l], gpost[l, 1], w_out, l, seq)
        xt = _ffn(xt, mod[l], gpre[l, 2], gpost[l, 2], wg, wu, wd, l, 1, seq)
    return xt.reshape(n_batch, seq, d)
```

```python
import functools
import math

import numpy as np
import jax
import jax.numpy as jnp
from jax import lax
from jax.experimental import pallas as pl
from jax.experimental.pallas import tpu as pltpu

HEAD_DIM = 64
LANES = 128
SWA_BLOCK = 128
SWA_WINDOW = 128
MOBA_BLOCK = 256
MOBA_TOPK = 3
FAR_GROUPS = (8, 2, 1)
ONES_ROWS = 16
T5_BUCKETS = 32
T5_MAX_DIST = 128
N_MOD = 9
EPS = 1e-6
NEG = -1e30
VMEM_LIMIT = 56 * 1024 * 1024

F32 = jnp.float32
BF16 = jnp.bfloat16


def _cparams(n_axes):
    return pltpu.CompilerParams(
        dimension_semantics=("arbitrary",) * n_axes,
        vmem_limit_bytes=VMEM_LIMIT,
    )


def _rms(x, g):
    return x * lax.rsqrt(jnp.mean(x * x, axis=-1, keepdims=True) + EPS) * g


def _dot(a, b):
    return jnp.dot(a, b, preferred_element_type=F32)


def _dot_nt(a, b):
    return lax.dot_general(a, b, (((1,), (1,)), ((), ())), preferred_element_type=F32)


def _mod_kernel(ct_ref, w_ref, b_ref, o_ref, *, n_batch, row_chunk):
    ct = ct_ref[...]
    ca = ct * jax.nn.sigmoid(ct)
    d = ct.shape[0]
    for b in range(n_batch):
        acc = b_ref[0]
        for r in range(0, d, row_chunk):
            w = w_ref[0, r:r + row_chunk, :]
            acc = acc + jnp.sum(w * ca[r:r + row_chunk, b:b + 1], axis=0, keepdims=True)
        o_ref[0, b:b + 1, :] = acc


def _modulation(c, ada_w, ada_b, *, tn=1152, row_chunk=128):
    depth, d, n = ada_w.shape
    n_batch = c.shape[0]
    return pl.pallas_call(
        functools.partial(_mod_kernel, n_batch=n_batch, row_chunk=row_chunk),
        grid=(depth, n // tn),
        in_specs=[
            pl.BlockSpec((d, n_batch), lambda l, j: (0, 0)),
            pl.BlockSpec((1, d, tn), lambda l, j: (l, 0, j)),
            pl.BlockSpec((1, 1, tn), lambda l, j: (l, 0, j)),
        ],
        out_specs=pl.BlockSpec((1, n_batch, tn), lambda l, j: (l, 0, j)),
        out_shape=jax.ShapeDtypeStruct((depth, n_batch, n), F32),
        compiler_params=_cparams(2),
        name="adaln_modulation",
    )(c.T, ada_w, ada_b.reshape(depth, 1, n))


def _modulate(x, gpre, mod, base):
    return (_rms(x, gpre) * (1.0 + mod[base + 1:base + 2]) + mod[base:base + 1]).astype(BF16)


def _swiglu_step(x, mod, base, gpre, gpost, wg_ref, wu_ref, wd_ref):
    h = _modulate(x, gpre, mod, base)
    g = _dot(h, wg_ref[0, 0])
    u = _dot(h, wu_ref[0, 0])
    y = _dot((g * jax.nn.sigmoid(g) * u).astype(BF16), wd_ref[0, 0])
    return x + 0.5 * mod[base + 2:base + 3] * _rms(y, gpost)


def _ffn_specs(d, d_ff, layer, which):
    const = pl.Buffered(1)
    return [
        pl.BlockSpec((1, 1, d, d_ff), lambda i: (layer, which, 0, 0), pipeline_mode=const),
        pl.BlockSpec((1, 1, d, d_ff), lambda i: (layer, which, 0, 0), pipeline_mode=const),
        pl.BlockSpec((1, 1, d_ff, d), lambda i: (layer, which, 0, 0), pipeline_mode=const),
    ]


def _vec_spec(width):
    return pl.BlockSpec((1, 1, width), lambda i: (0, 0, 0))


def _ffn_qkv_kernel(x_ref, mod_ref, gpre0_ref, gpost0_ref, wg_ref, wu_ref, wd_ref,
                    gpre1_ref, w_ref, b_ref, x_out_ref, qkv_ref):
    mod = mod_ref[0]
    x = _swiglu_step(x_ref[...], mod, 0, gpre0_ref[0], gpost0_ref[0], wg_ref, wu_ref, wd_ref)
    x_out_ref[...] = x
    h = _modulate(x, gpre1_ref[0], mod, 3)
    qkv_ref[...] = (_dot(h, w_ref[0]) + b_ref[0]).astype(qkv_ref.dtype)


def _ffn_qkv(x, mod_l, gpre0, gpost0, wg, wu, wd, gpre1, w_in, b_in, layer, seq, *, tm=512):
    t, d = x.shape
    d_ff = wg.shape[-1]
    n = w_in.shape[-1]
    tiles_per_batch = seq // tm
    return pl.pallas_call(
        _ffn_qkv_kernel,
        grid=(t // tm,),
        in_specs=[
            pl.BlockSpec((tm, d), lambda i: (i, 0)),
            pl.BlockSpec((1, N_MOD, d), lambda i: (i // tiles_per_batch, 0, 0)),
            _vec_spec(d), _vec_spec(d),
            *_ffn_specs(d, d_ff, layer, 0),
            _vec_spec(d),
            pl.BlockSpec((1, d, n), lambda i: (layer, 0, 0), pipeline_mode=pl.Buffered(1)),
            pl.BlockSpec((1, 1, n), lambda i: (layer, 0, 0)),
        ],
        out_specs=[pl.BlockSpec((tm, d), lambda i: (i, 0)),
                   pl.BlockSpec((tm, n), lambda i: (i, 0))],
        out_shape=[jax.ShapeDtypeStruct((t, d), F32), jax.ShapeDtypeStruct((t, n), BF16)],
        compiler_params=_cparams(1),
        name="swiglu_qkv",
    )(x, mod_l, gpre0, gpost0, wg, wu, wd, gpre1, w_in, b_in)


def _swa_kernel(q_ref, kp_ref, km_ref, vp_ref, vm_ref, bias_ref, sink_ref, o_ref, *, group):
    tq = q_ref.shape[0]
    W = 2 * SWA_BLOCK
    i = pl.program_id(1)
    k_all = jnp.concatenate([kp_ref[...], km_ref[...]], axis=0)
    v_all = jnp.concatenate([vp_ref[...], vm_ref[...]], axis=0)
    vt = jnp.concatenate([v_all.astype(F32).T.astype(BF16),
                          jnp.ones((ONES_ROWS, v_all.shape[0]), BF16)], axis=0)
    feat_lo = lax.broadcasted_iota(jnp.int32, (LANES, SWA_BLOCK), 0) < HEAD_DIM
    key_row = lax.broadcasted_iota(jnp.int32, (W, group * SWA_BLOCK), 0)
    no_prev = jnp.logical_and(key_row < SWA_BLOCK, i == 0)
    scale = HEAD_DIM ** -0.5

    def swap_halves(x):
        return jnp.concatenate([x[HEAD_DIM:], x[:HEAD_DIM]], axis=0)

    for blk in range(tq // SWA_BLOCK):
        r0 = blk * SWA_BLOCK
        kw = k_all[r0:r0 + W]
        vtw = vt[:, r0:r0 + W]
        for kv in range(2):
            cols = []
            for g in range(group):
                h = kv * group + g
                pair, odd = h // 2, h % 2
                qpt = (q_ref[r0:r0 + SWA_BLOCK, pair * LANES:(pair + 1) * LANES].astype(F32)
                       * scale).T
                if odd != kv:
                    qpt = swap_halves(qpt)
                keep = feat_lo if kv == 0 else jnp.logical_not(feat_lo)
                cols.append(jnp.where(keep, qpt, jnp.zeros_like(qpt)))
            qt = jnp.concatenate(cols, axis=1).astype(BF16)
            st = _dot(kw, qt)
            st = st + bias_ref[kv]
            if blk == 0:
                st = jnp.where(no_prev, NEG, st)
            sink = sink_ref[kv]
            m = jnp.maximum(jnp.max(st, axis=0, keepdims=True), sink)
            acc = _dot(vtw, jnp.exp(st - m).astype(BF16))
            out_t = acc[:LANES] * (1.0 / (acc[LANES:LANES + 1] + jnp.exp(sink - m)))
            f0 = kv * HEAD_DIM
            for gp in range(group // 2):
                c0 = 2 * gp * SWA_BLOCK
                pair_t = jnp.concatenate(
                    [out_t[f0:f0 + HEAD_DIM, c0:c0 + SWA_BLOCK],
                     out_t[f0:f0 + HEAD_DIM, c0 + SWA_BLOCK:c0 + 2 * SWA_BLOCK]], axis=0)
                pair = (kv * group) // 2 + gp
                o_ref[r0:r0 + SWA_BLOCK, pair * LANES:(pair + 1) * LANES] = pair_t.T.astype(
                    o_ref.dtype)


def _swa(qkv, bias, sinks, n_batch, seq, n_q_heads, n_kv_heads, *, tq=512):
    t = qkv.shape[0]
    nq = seq // tq
    qw = n_q_heads * HEAD_DIM
    k_col = qw // LANES
    v_col = k_col + (n_kv_heads * HEAD_DIM) // LANES
    per = tq // SWA_BLOCK
    group = n_q_heads // n_kv_heads
    assert n_kv_heads * HEAD_DIM == LANES and group % 2 == 0

    def prev_map(col):
        return lambda b, i: (jnp.maximum((b * nq + i) * per - 1, 0), col)

    return pl.pallas_call(
        functools.partial(_swa_kernel, group=group),
        grid=(n_batch, nq),
        in_specs=[
            pl.BlockSpec((tq, qw), lambda b, i: (b * nq + i, 0)),
            pl.BlockSpec((SWA_BLOCK, LANES), prev_map(k_col)),
            pl.BlockSpec((tq, LANES), lambda b, i: (b * nq + i, k_col)),
            pl.BlockSpec((SWA_BLOCK, LANES), prev_map(v_col)),
            pl.BlockSpec((tq, LANES), lambda b, i: (b * nq + i, v_col)),
            pl.BlockSpec((n_kv_heads, 2 * SWA_BLOCK, group * SWA_BLOCK), lambda b, i: (0, 0, 0)),
            pl.BlockSpec((n_kv_heads, 1, group * SWA_BLOCK), lambda b, i: (0, 0, 0)),
        ],
        out_specs=pl.BlockSpec((tq, qw), lambda b, i: (b * nq + i, 0)),
        out_shape=jax.ShapeDtypeStruct((t, qw), BF16),
        compiler_params=_cparams(2),
        name="swa_attention",
    )(qkv, qkv, qkv, qkv, qkv, bias, sinks)


def _moba_kernel(q_ref, k_ref, v_ref, bown_ref, bprev_ref, bconst_ref, o_ref, kmean_ref, vt_ref,
                 st_ref, *, n_blocks):
    L = MOBA_BLOCK
    c = pl.program_id(2)

    @pl.when(c == 0)
    def _():
        kmean_ref[...] = jnp.zeros_like(kmean_ref)
        for jb in range(n_blocks):
            kb = k_ref[jb * L:(jb + 1) * L, :].astype(F32)
            kmean_ref[jb:jb + 1, :] = jnp.sum(kb, axis=0, keepdims=True) * (1.0 / L)
            vt_ref[jb, :LANES, :] = v_ref[jb * L:(jb + 1) * L, :].astype(F32).T.astype(BF16)
            vt_ref[jb, LANES:, :] = jnp.ones((ONES_ROWS, L), BF16)

    scale = HEAD_DIM ** -0.5
    qt = (q_ref[...].astype(F32) * scale).T
    lo = lax.broadcasted_iota(jnp.int32, (LANES, L), 0) < HEAD_DIM
    zero = jnp.zeros_like(qt)
    q2t = jnp.concatenate([jnp.where(lo, qt, zero), jnp.where(lo, zero, qt)],
                          axis=1).astype(BF16)
    cols = 2 * L

    km = kmean_ref[...]
    km_hi = km.astype(BF16)
    km_lo = (km - km_hi.astype(F32)).astype(BF16)
    gate = _dot(km_hi, q2t) + _dot(km_lo, q2t)
    rowf = lax.broadcasted_iota(jnp.int32, gate.shape, 0).astype(F32)
    g = jnp.where(rowf < c.astype(F32), gate, NEG)
    sels = []
    for t in range(MOBA_TOPK):
        gmax = jnp.max(g, axis=0, keepdims=True)
        idx = jnp.min(jnp.where(g == gmax, rowf, float(n_blocks)), axis=0, keepdims=True)
        sels.append(jnp.where(t < c, idx, -1.0))
        g = jnp.where(rowf == idx, -jnp.inf, g)

    def selected(jf):
        hit = sels[0] == jf
        for s_t in sels[1:]:
            hit = jnp.logical_or(hit, s_t == jf)
        return hit

    def scores(j, n=1):
        return _dot(k_ref[pl.ds(pl.multiple_of(j * L, L), n * L), :], q2t)

    def colmax(st):
        return jnp.max(st, axis=0, keepdims=True)

    st_own = scores(c) + bown_ref[0]
    jp = jnp.maximum(c - 1, 0)
    sel_prev = jnp.logical_and(selected((c - 1).astype(F32)), c > 0)
    st_prev = jnp.where(sel_prev, scores(jp) + bprev_ref[0], NEG)
    m_near = jnp.maximum(colmax(st_own), colmax(st_prev))

    def far_scores(j0, n, m_far):
        st = scores(j0, n)
        for g in range(n):
            stg = jnp.where(selected((j0 + g).astype(F32)), st[g * L:(g + 1) * L], NEG)
            st_ref[j0 + g] = stg
            m_far = jnp.maximum(m_far, colmax(stg))
        return m_far

    n_far = jnp.maximum(c - 1, 0)

    def far_sweep(fn, carry):
        start = jnp.int32(0)
        for n in FAR_GROUPS:
            count = lax.shift_right_logical(n_far - start, int(math.log2(n)))
            carry = lax.fori_loop(0, count, lambda i, cr, s=start, n=n: fn(s + i * n, n, cr),
                                  carry)
            start = start + count * n
        return carry

    m_far = far_sweep(far_scores, jnp.full((1, cols), NEG, F32))
    bconst = bconst_ref[0]
    m = jnp.maximum(m_near, m_far + bconst)

    def pv(j, p):
        return _dot(vt_ref[j], p.astype(BF16))

    acc = pv(c, jnp.exp(st_own - m)) + pv(jp, jnp.exp(st_prev - m))
    m_off = m - bconst

    def far_pv(j0, n, acc):
        for g in range(n):
            acc = acc + pv(j0 + g, jnp.exp((st_ref[j0 + g] - m_off).astype(BF16)))
        return acc

    acc = far_sweep(far_pv, acc)
    out_t = acc[:LANES] * (1.0 / acc[LANES:LANES + 1])
    o_ref[...] = jnp.where(lo, out_t[:, :L], out_t[:, L:]).T.astype(o_ref.dtype)


def _moba(qkv, bown_t, bprev_t, bconst, n_batch, seq, q_col, k_col, v_col, n_heads):
    t = qkv.shape[0]
    L = MOBA_BLOCK
    nb = seq // L
    nbp = -(-nb // 8) * 8
    n_pairs = n_heads // 2
    return pl.pallas_call(
        functools.partial(_moba_kernel, n_blocks=nb),
        grid=(n_batch, n_pairs, nb),
        in_specs=[
            pl.BlockSpec((L, LANES), lambda b, p, c: (b * nb + c, q_col + p)),
            pl.BlockSpec((seq, LANES), lambda b, p, c: (b, k_col + p)),
            pl.BlockSpec((seq, LANES), lambda b, p, c: (b, v_col + p)),
            pl.BlockSpec((1, L, 2 * L), lambda b, p, c: (p, 0, 0)),
            pl.BlockSpec((1, L, 2 * L), lambda b, p, c: (p, 0, 0)),
            pl.BlockSpec((1, 1, 2 * L), lambda b, p, c: (p, 0, 0)),
        ],
        out_specs=pl.BlockSpec((L, LANES), lambda b, p, c: (b * nb + c, p)),
        out_shape=jax.ShapeDtypeStruct((t, n_heads * HEAD_DIM), BF16),
        scratch_shapes=[pltpu.VMEM((nbp, LANES), F32),
                        pltpu.VMEM((nb, LANES + ONES_ROWS, L), BF16),
                        pltpu.VMEM((nb, L, 2 * L), F32)],
        compiler_params=_cparams(3),
        name="moba_attention",
    )(qkv, qkv, qkv, bown_t, bprev_t, bconst)


def _outproj_ffn_kernel(x_ref, ya_ref, yb_ref, mod_ref, gg_ref, gpost1_ref, w_ref,
                        gpre2_ref, gpost2_ref, wg_ref, wu_ref, wd_ref, o_ref):
    wa = ya_ref.shape[1]
    mod = mod_ref[0]
    gg = gg_ref[0]
    ya = _rms(ya_ref[...].astype(F32), gg[:, :wa]).astype(BF16)
    yb = _rms(yb_ref[...].astype(F32), gg[:, wa:]).astype(BF16)
    y = _dot(ya, w_ref[0, :wa, :]) + _dot(yb, w_ref[0, wa:, :])
    x = x_ref[...] + mod[5:6] * _rms(y, gpost1_ref[0])
    o_ref[...] = _swiglu_step(x, mod, 6, gpre2_ref[0], gpost2_ref[0], wg_ref, wu_ref, wd_ref)


def _outproj_ffn(x, ya, yb, mod_l, gg, gpost1, w_out, gpre2, gpost2, wg, wu, wd, layer, seq,
                 *, tm=512):
    t, d = x.shape
    d_ff = wg.shape[-1]
    wa, wb = ya.shape[1], yb.shape[1]
    tiles_per_batch = seq // tm
    return pl.pallas_call(
        _outproj_ffn_kernel,
        grid=(t // tm,),
        in_specs=[
            pl.BlockSpec((tm, d), lambda i: (i, 0)),
            pl.BlockSpec((tm, wa), lambda i: (i, 0)),
            pl.BlockSpec((tm, wb), lambda i: (i, 0)),
            pl.BlockSpec((1, N_MOD, d), lambda i: (i // tiles_per_batch, 0, 0)),
            _vec_spec(wa + wb), _vec_spec(d),
            pl.BlockSpec((1, wa + wb, d), lambda i: (layer, 0, 0), pipeline_mode=pl.Buffered(1)),
            _vec_spec(d), _vec_spec(d),
            *_ffn_specs(d, d_ff, layer, 1),
        ],
        out_specs=pl.BlockSpec((tm, d), lambda i: (i, 0)),
        out_shape=jax.ShapeDtypeStruct((t, d), F32),
        compiler_params=_cparams(1),
        name="outproj_swiglu",
    )(x, ya, yb, mod_l, gg, gpost1, w_out, gpre2, gpost2, wg, wu, wd)


def _t5_bucket_np(dist):
    n = np.maximum(dist, 0)
    max_exact = T5_BUCKETS // 2
    nf = np.maximum(n, 1).astype(np.float32)
    large = max_exact + (np.log(nf / np.float32(max_exact)) / np.float32(math.log(T5_MAX_DIST / max_exact))
                         * np.float32(T5_BUCKETS - max_exact)).astype(np.int32)
    large = np.minimum(large, T5_BUCKETS - 1)
    return np.where(n < max_exact, n, large).astype(np.int32)


def _bias_kernel(tab_ref, *refs, head_base, heads_per_step):
    n = len(refs) // 2
    step = pl.program_id(0)
    for idx_ref, o_ref in zip(refs[:n], refs[n:]):
        idx = idx_ref[...]
        width = idx.shape[1]
        for hh in range(heads_per_step):
            head = head_base + step * heads_per_step + hh
            out = jnp.full(idx.shape, NEG, F32)
            for bucket in range(T5_BUCKETS):
                out = jnp.where(idx == bucket, tab_ref[bucket, head], out)
            o_ref[0, :, hh * width:(hh + 1) * width] = out


def _bias_expand(rel_bias, buckets, head_base, n_steps, heads_per_step):
    shapes = [b.shape for b in buckets]
    return pl.pallas_call(
        functools.partial(_bias_kernel, head_base=head_base, heads_per_step=heads_per_step),
        grid=(n_steps,),
        in_specs=[pl.BlockSpec(memory_space=pltpu.SMEM)]
        + [pl.BlockSpec(s, lambda h: (0, 0)) for s in shapes],
        out_specs=[pl.BlockSpec((1, s[0], heads_per_step * s[1]), lambda h: (h, 0, 0))
                   for s in shapes],
        out_shape=[jax.ShapeDtypeStruct((n_steps, s[0], heads_per_step * s[1]), F32)
                   for s in shapes],
        compiler_params=_cparams(1),
        name="t5_bias_tables",
    )(rel_bias, *[jnp.asarray(b) for b in buckets])


def _bias_tables(rel_bias, n_swa_heads, n_swa_kv_heads):
    n_heads = rel_bias.shape[1]
    kj = np.arange(2 * SWA_BLOCK)[:, None]
    qi = np.arange(SWA_BLOCK)[None, :]
    dist = qi + SWA_BLOCK - kj
    swa_idx = np.where((dist >= 0) & (dist < SWA_WINDOW), _t5_bucket_np(dist), -1).astype(np.int32)
    (swa,) = _bias_expand(rel_bias, [swa_idx], 0, n_swa_kv_heads,
                          n_swa_heads // n_swa_kv_heads)

    kk = np.arange(MOBA_BLOCK)[:, None]
    qq = np.arange(MOBA_BLOCK)[None, :]
    d_own = qq - kk
    own_idx = np.where(d_own >= 0, _t5_bucket_np(d_own), -1).astype(np.int32)
    prev_idx = _t5_bucket_np(d_own + MOBA_BLOCK)
    n_pairs = (n_heads - n_swa_heads) // 2
    own_t, prev_t = _bias_expand(rel_bias, [own_idx, prev_idx], n_swa_heads, n_pairs, 2)
    far = int(_t5_bucket_np(np.array(MOBA_BLOCK + 1)))
    assert far == int(_t5_bucket_np(np.array(1 << 30)))
    const = jnp.repeat(rel_bias[far, n_swa_heads:], MOBA_BLOCK).reshape(n_pairs, 1, 2 * MOBA_BLOCK)
    return swa, own_t, prev_t, const


def kernel(x, c, rel_bias, ada_w, ada_b, norm_pre, norm_post, ffn_w_gate, ffn_w_up, ffn_w_down,
           mix_w_in, mix_b_in, mix_w_out, attn_sinks, group_gain):
    n_batch, seq, d = x.shape
    depth = ada_w.shape[0]
    n_heads = d // HEAD_DIM
    swa_q_heads = n_heads // 2
    swa_kv_heads = max(1, swa_q_heads // 4)
    moba_heads = n_heads - swa_q_heads
    swa_w = swa_q_heads * HEAD_DIM
    kv_w = swa_kv_heads * HEAD_DIM
    moba_w = moba_heads * HEAD_DIM
    q_col = (swa_w + 2 * kv_w) // LANES
    k_col = q_col + moba_w // LANES
    v_col = k_col + moba_w // LANES

    mod = _modulation(c, ada_w, ada_b).reshape(depth, n_batch, N_MOD, d)
    bias_swa, bias_own_t, bias_prev_t, bias_const = _bias_tables(rel_bias, swa_q_heads, swa_kv_heads)
    sinks = jnp.repeat(attn_sinks, SWA_BLOCK, axis=1).reshape(depth, swa_kv_heads, 1, -1)

    wg = ffn_w_gate.astype(BF16)
    wu = ffn_w_up.astype(BF16)
    wd = ffn_w_down.astype(BF16)
    w_in = mix_w_in.astype(BF16)
    w_out = mix_w_out.astype(BF16)
    b_in = mix_b_in[:, None, :]
    gpre = norm_pre[:, :, None, None, :]
    gpost = norm_post[:, :, None, None, :]
    gg = group_gain[:, None, None, :]

    xt = x.reshape(n_batch * seq, d)
    for l in range(depth):
        xt, qkv = _ffn_qkv(xt, mod[l], gpre[l, 0], gpost[l, 0], wg, wu, wd, gpre[l, 1], w_in, b_in,
                           l, seq)
        ya = _swa(qkv, bias_swa, sinks[l], n_batch, seq, swa_q_heads, swa_kv_heads)
        yb = _moba(qkv, bias_own_t, bias_prev_t, bias_const, n_batch, seq, q_col, k_col, v_col,
                   moba_heads)
        xt = _outproj_ffn(xt, ya, yb, mod[l], gg[l], gpost[l, 1], w_out, gpre[l, 2], gpost[l, 2],
                          wg, wu, wd, l, seq)
    return xt.reshape(n_batch, seq, d)
```

```python
import functools
import math

import numpy as np
import jax
import jax.numpy as jnp
from jax import lax
from jax.experimental import pallas as pl
from jax.experimental.pallas import tpu as pltpu

HEAD_DIM = 64
LANES = 128
SWA_BLOCK = 128
SWA_WINDOW = 128
MOBA_BLOCK = 256
MOBA_TOPK = 3
FAR_GROUPS = (8, 4, 2, 1)
ONES_ROWS = 16
T5_BUCKETS = 32
T5_MAX_DIST = 128
N_MOD = 9
EPS = 1e-6
NEG = -1e30
VMEM_LIMIT = 56 * 1024 * 1024

F32 = jnp.float32
BF16 = jnp.bfloat16


def _cparams(n_axes):
    return pltpu.CompilerParams(
        dimension_semantics=("arbitrary",) * n_axes,
        vmem_limit_bytes=VMEM_LIMIT,
    )


def _rms(x, g):
    return x * lax.rsqrt(jnp.mean(x * x, axis=-1, keepdims=True) + EPS) * g


def _dot(a, b):
    return jnp.dot(a, b, preferred_element_type=F32)


def _dot_nt(a, b):
    return lax.dot_general(a, b, (((1,), (1,)), ((), ())), preferred_element_type=F32)


def _mod_kernel(ct_ref, w_ref, b_ref, o_ref, *, n_batch, row_chunk):
    ct = ct_ref[...]
    ca = ct * jax.nn.sigmoid(ct)
    d = ct.shape[0]
    for b in range(n_batch):
        acc = b_ref[0]
        for r in range(0, d, row_chunk):
            w = w_ref[0, r:r + row_chunk, :]
            acc = acc + jnp.sum(w * ca[r:r + row_chunk, b:b + 1], axis=0, keepdims=True)
        o_ref[0, b:b + 1, :] = acc


def _modulation(c, ada_w, ada_b, *, tn=1152, row_chunk=128):
    depth, d, n = ada_w.shape
    n_batch = c.shape[0]
    return pl.pallas_call(
        functools.partial(_mod_kernel, n_batch=n_batch, row_chunk=row_chunk),
        grid=(depth, n // tn),
        in_specs=[
            pl.BlockSpec((d, n_batch), lambda l, j: (0, 0)),
            pl.BlockSpec((1, d, tn), lambda l, j: (l, 0, j)),
            pl.BlockSpec((1, 1, tn), lambda l, j: (l, 0, j)),
        ],
        out_specs=pl.BlockSpec((1, n_batch, tn), lambda l, j: (l, 0, j)),
        out_shape=jax.ShapeDtypeStruct((depth, n_batch, n), F32),
        compiler_params=_cparams(2),
        name="adaln_modulation",
    )(c.T, ada_w, ada_b.reshape(depth, 1, n))


def _modulate(x, gpre, mod, base):
    return (_rms(x, gpre * (1.0 + mod[base + 1:base + 2])) + mod[base:base + 1]).astype(BF16)


def _swiglu_step(x, mod, base, gpre, gpost, wg_ref, wu_ref, wd_ref):
    h = _modulate(x, gpre, mod, base)
    g = _dot(h, wg_ref[0, 0])
    u = _dot(h, wu_ref[0, 0])
    y = _dot((g * jax.nn.sigmoid(g) * u).astype(BF16), wd_ref[0, 0])
    return x + _rms(y, gpost * (0.5 * mod[base + 2:base + 3]))


def _ffn_specs(d, d_ff, layer, which):
    const = pl.Buffered(1)
    return [
        pl.BlockSpec((1, 1, d, d_ff), lambda i: (layer, which, 0, 0), pipeline_mode=const),
        pl.BlockSpec((1, 1, d, d_ff), lambda i: (layer, which, 0, 0), pipeline_mode=const),
        pl.BlockSpec((1, 1, d_ff, d), lambda i: (layer, which, 0, 0), pipeline_mode=const),
    ]


def _vec_spec(width):
    return pl.BlockSpec((1, 1, width), lambda i: (0, 0, 0))


def _ffn_kernel(x_ref, mod_ref, gpre_ref, gpost_ref, wg_ref, wu_ref, wd_ref, o_ref):
    o_ref[...] = _swiglu_step(x_ref[...], mod_ref[0], 0, gpre_ref[0], gpost_ref[0],
                              wg_ref, wu_ref, wd_ref)


def _ffn(x, mod_l, gpre, gpost, wg, wu, wd, layer, seq, *, tm=512):
    t, d = x.shape
    tiles_per_batch = seq // tm
    return pl.pallas_call(
        _ffn_kernel,
        grid=(t // tm,),
        in_specs=[
            pl.BlockSpec((tm, d), lambda i: (i, 0)),
            pl.BlockSpec((1, N_MOD, d), lambda i: (i // tiles_per_batch, 0, 0)),
            _vec_spec(d), _vec_spec(d),
            *_ffn_specs(d, wg.shape[-1], layer, 0),
        ],
        out_specs=pl.BlockSpec((tm, d), lambda i: (i, 0)),
        out_shape=jax.ShapeDtypeStruct((t, d), F32),
        compiler_params=_cparams(1),
        name="swiglu_half_step",
    )(x, mod_l, gpre, gpost, wg, wu, wd)


def _inproj_kernel(x_ref, mod_ref, gpre_ref, w_ref, b_ref, o_ref):
    h = _modulate(x_ref[...], gpre_ref[0], mod_ref[0], 3)
    o_ref[...] = (_dot(h, w_ref[0]) + b_ref[0]).astype(o_ref.dtype)


def _inproj(x, mod_l, gpre, w_in, b_in, layer, seq, *, tm=512):
    t, d = x.shape
    n = w_in.shape[-1]
    tiles_per_batch = seq // tm
    return pl.pallas_call(
        _inproj_kernel,
        grid=(t // tm,),
        in_specs=[
            pl.BlockSpec((tm, d), lambda i: (i, 0)),
            pl.BlockSpec((1, N_MOD, d), lambda i: (i // tiles_per_batch, 0, 0)),
            _vec_spec(d),
            pl.BlockSpec((1, d, n), lambda i: (layer, 0, 0), pipeline_mode=pl.Buffered(1)),
            pl.BlockSpec((1, 1, n), lambda i: (layer, 0, 0)),
        ],
        out_specs=pl.BlockSpec((tm, n), lambda i: (i, 0)),
        out_shape=jax.ShapeDtypeStruct((t, n), BF16),
        compiler_params=_cparams(1),
        name="qkv_projection",
    )(x, mod_l, gpre, w_in, b_in)


def _swa_kernel(q_ref, kp_ref, km_ref, vp_ref, vm_ref, bias_ref, sink_ref, o_ref, *, group):
    tq = q_ref.shape[0]
    W = 2 * SWA_BLOCK
    i = pl.program_id(1)
    k_all = jnp.concatenate([kp_ref[...], km_ref[...]], axis=0)
    v_all = jnp.concatenate([vp_ref[...], vm_ref[...]], axis=0)
    vt_all = v_all.astype(F32).T.astype(BF16)
    ones = jnp.ones((ONES_ROWS, v_all.shape[0]), BF16)
    vt = [jnp.concatenate([vt_all[kv * HEAD_DIM:(kv + 1) * HEAD_DIM], ones], axis=0)
          for kv in range(2)]
    feat_lo = lax.broadcasted_iota(jnp.int32, (LANES, SWA_BLOCK), 0) < HEAD_DIM
    key_row = lax.broadcasted_iota(jnp.int32, (W, group * SWA_BLOCK), 0)
    no_prev = jnp.logical_and(key_row < SWA_BLOCK, i == 0)
    scale = HEAD_DIM ** -0.5

    def swap_halves(x):
        return jnp.concatenate([x[HEAD_DIM:], x[:HEAD_DIM]], axis=0)

    for blk in range(tq // SWA_BLOCK):
        r0 = blk * SWA_BLOCK
        kw = k_all[r0:r0 + W]
        for kv in range(2):
            cols = []
            for g in range(group):
                h = kv * group + g
                pair, odd = h // 2, h % 2
                qpt = (q_ref[r0:r0 + SWA_BLOCK, pair * LANES:(pair + 1) * LANES].astype(F32)
                       * scale).T
                if odd != kv:
                    qpt = swap_halves(qpt)
                keep = feat_lo if kv == 0 else jnp.logical_not(feat_lo)
                cols.append(jnp.where(keep, qpt, jnp.zeros_like(qpt)))
            qt = jnp.concatenate(cols, axis=1).astype(BF16)
            st = _dot(kw, qt)
            st = st + bias_ref[kv]
            if blk == 0:
                st = jnp.where(no_prev, NEG, st)
            sink = sink_ref[kv]
            m = jnp.maximum(jnp.max(st, axis=0, keepdims=True), sink)
            acc = _dot(vt[kv][:, r0:r0 + W], jnp.exp(st - m).astype(BF16))
            out_t = acc[:HEAD_DIM] * (1.0 / (acc[HEAD_DIM:HEAD_DIM + 1] + jnp.exp(sink - m)))
            for gp in range(group // 2):
                c0 = 2 * gp * SWA_BLOCK
                pair_t = jnp.concatenate(
                    [out_t[:, c0:c0 + SWA_BLOCK], out_t[:, c0 + SWA_BLOCK:c0 + 2 * SWA_BLOCK]],
                    axis=0)
                pair = (kv * group) // 2 + gp
                o_ref[r0:r0 + SWA_BLOCK, pair * LANES:(pair + 1) * LANES] = pair_t.T.astype(
                    o_ref.dtype)


def _swa(qkv, bias, sinks, n_batch, seq, n_q_heads, n_kv_heads, *, tq=512):
    t = qkv.shape[0]
    nq = seq // tq
    qw = n_q_heads * HEAD_DIM
    k_col = qw // LANES
    v_col = k_col + (n_kv_heads * HEAD_DIM) // LANES
    per = tq // SWA_BLOCK
    group = n_q_heads // n_kv_heads
    assert n_kv_heads * HEAD_DIM == LANES and group % 2 == 0

    def prev_map(col):
        return lambda b, i: (jnp.maximum((b * nq + i) * per - 1, 0), col)

    return pl.pallas_call(
        functools.partial(_swa_kernel, group=group),
        grid=(n_batch, nq),
        in_specs=[
            pl.BlockSpec((tq, qw), lambda b, i: (b * nq + i, 0)),
            pl.BlockSpec((SWA_BLOCK, LANES), prev_map(k_col)),
            pl.BlockSpec((tq, LANES), lambda b, i: (b * nq + i, k_col)),
            pl.BlockSpec((SWA_BLOCK, LANES), prev_map(v_col)),
            pl.BlockSpec((tq, LANES), lambda b, i: (b * nq + i, v_col)),
            pl.BlockSpec((n_kv_heads, 2 * SWA_BLOCK, group * SWA_BLOCK), lambda b, i: (0, 0, 0)),
            pl.BlockSpec((n_kv_heads, 1, group * SWA_BLOCK), lambda b, i: (0, 0, 0)),
        ],
        out_specs=pl.BlockSpec((tq, qw), lambda b, i: (b * nq + i, 0)),
        out_shape=jax.ShapeDtypeStruct((t, qw), BF16),
        compiler_params=_cparams(2),
        name="swa_attention",
    )(qkv, qkv, qkv, qkv, qkv, bias, sinks)


def _moba_kernel(q_ref, k_ref, v_ref, bown_ref, bprev_ref, bconst_ref, o_ref, kmean_ref, vt_ref,
                 st_ref, *, n_blocks):
    L = MOBA_BLOCK
    c = pl.program_id(2)

    @pl.when(c == 0)
    def _():
        kmean_ref[...] = jnp.zeros_like(kmean_ref)
        for jb in range(n_blocks):
            kb = k_ref[jb * L:(jb + 1) * L, :].astype(F32)
            kmean_ref[jb:jb + 1, :] = jnp.sum(kb, axis=0, keepdims=True) * (1.0 / L)
            vt = v_ref[jb * L:(jb + 1) * L, :].astype(F32).T.astype(BF16)
            for head in range(2):
                vt_ref[jb, head, :HEAD_DIM, :] = vt[head * HEAD_DIM:(head + 1) * HEAD_DIM]
                vt_ref[jb, head, HEAD_DIM:, :] = jnp.ones((ONES_ROWS, L), BF16)

    scale = HEAD_DIM ** -0.5
    qt = (q_ref[...].astype(F32) * scale).T
    lo = lax.broadcasted_iota(jnp.int32, (LANES, L), 0) < HEAD_DIM
    zero = jnp.zeros_like(qt)
    q2t = jnp.concatenate([jnp.where(lo, qt, zero), jnp.where(lo, zero, qt)],
                          axis=1).astype(BF16)
    cols = 2 * L

    km = kmean_ref[...]
    km_hi = km.astype(BF16)
    km_lo = (km - km_hi.astype(F32)).astype(BF16)
    gate = _dot(km_hi, q2t) + _dot(km_lo, q2t)
    rowf = lax.broadcasted_iota(jnp.int32, gate.shape, 0).astype(F32)
    g = jnp.where(rowf < c.astype(F32), gate, NEG)
    sels = []
    for t in range(MOBA_TOPK):
        gmax = jnp.max(g, axis=0, keepdims=True)
        idx = jnp.min(jnp.where(g == gmax, rowf, float(n_blocks)), axis=0, keepdims=True)
        sels.append(jnp.where(t < c, idx, -1.0))
        g = jnp.where(rowf == idx, -jnp.inf, g)

    def selected(jf):
        hit = sels[0] == jf
        for s_t in sels[1:]:
            hit = jnp.logical_or(hit, s_t == jf)
        return hit

    def scores(j, n=1):
        return _dot(k_ref[pl.ds(pl.multiple_of(j * L, L), n * L), :], q2t)

    def colmax(st):
        return jnp.max(st, axis=0, keepdims=True)

    st_own = scores(c) + bown_ref[0]
    jp = jnp.maximum(c - 1, 0)
    sel_prev = jnp.logical_and(selected((c - 1).astype(F32)), c > 0)
    st_prev = jnp.where(sel_prev, scores(jp) + bprev_ref[0], NEG)
    m_near = jnp.maximum(colmax(st_own), colmax(st_prev))

    def far_scores(j0, n, m_far):
        st = scores(j0, n)
        for g in range(n):
            stg = jnp.where(selected((j0 + g).astype(F32)), st[g * L:(g + 1) * L], NEG)
            st_ref[j0 + g] = stg
            m_far = jnp.maximum(m_far, colmax(stg))
        return m_far

    n_far = jnp.maximum(c - 1, 0)

    def far_sweep(fn, carry):
        start = jnp.int32(0)
        for n in FAR_GROUPS:
            count = lax.shift_right_logical(n_far - start, int(math.log2(n)))
            carry = lax.fori_loop(0, count, lambda i, cr, s=start, n=n: fn(s + i * n, n, cr),
                                  carry)
            start = start + count * n
        return carry

    m_far = far_sweep(far_scores, jnp.full((1, cols), NEG, F32))
    bconst = bconst_ref[0]
    m = jnp.maximum(m_near, m_far + bconst)

    def add_pv(acc, j, x):
        p = jnp.exp(x.astype(BF16))
        pv = [_dot(vt_ref[j, head], p[:, head * L:(head + 1) * L]) for head in range(2)]
        return pv if acc is None else [a + b for a, b in zip(acc, pv)]

    acc = add_pv(add_pv(None, c, st_own - m), jp, st_prev - m)
    m_off = m - bconst

    def far_pv(j0, n, acc):
        for g in range(n):
            acc = add_pv(acc, j0 + g, st_ref[j0 + g] - m_off)
        return acc

    acc = far_sweep(far_pv, acc)
    out_t = jnp.concatenate([a[:HEAD_DIM] * (1.0 / a[HEAD_DIM:HEAD_DIM + 1]) for a in acc],
                            axis=0)
    o_ref[...] = out_t.T.astype(o_ref.dtype)


def _moba(qkv, bown_t, bprev_t, bconst, n_batch, seq, q_col, k_col, v_col, n_heads):
    t = qkv.shape[0]
    L = MOBA_BLOCK
    nb = seq // L
    nbp = -(-nb // 8) * 8
    n_pairs = n_heads // 2
    return pl.pallas_call(
        functools.partial(_moba_kernel, n_blocks=nb),
        grid=(n_batch, n_pairs, nb),
        in_specs=[
            pl.BlockSpec((L, LANES), lambda b, p, c: (b * nb + c, q_col + p)),
            pl.BlockSpec((seq, LANES), lambda b, p, c: (b, k_col + p)),
            pl.BlockSpec((seq, LANES), lambda b, p, c: (b, v_col + p)),
            pl.BlockSpec((1, L, 2 * L), lambda b, p, c: (p, 0, 0)),
            pl.BlockSpec((1, L, 2 * L), lambda b, p, c: (p, 0, 0)),
            pl.BlockSpec((1, 1, 2 * L), lambda b, p, c: (p, 0, 0)),
        ],
        out_specs=pl.BlockSpec((L, LANES), lambda b, p, c: (b * nb + c, p)),
        out_shape=jax.ShapeDtypeStruct((t, n_heads * HEAD_DIM), BF16),
        scratch_shapes=[pltpu.VMEM((nbp, LANES), F32),
                        pltpu.VMEM((nb, 2, HEAD_DIM + ONES_ROWS, L), BF16),
                        pltpu.VMEM((nb, L, 2 * L), F32)],
        compiler_params=_cparams(3),
        name="moba_attention",
    )(qkv, qkv, qkv, bown_t, bprev_t, bconst)


def _outproj_ffn_kernel(x_ref, ya_ref, yb_ref, mod_ref, gg_ref, gpost1_ref, w_ref,
                        gpre2_ref, gpost2_ref, wg_ref, wu_ref, wd_ref, o_ref):
    wa = ya_ref.shape[1]
    mod = mod_ref[0]
    gg = gg_ref[0]
    ya = _rms(ya_ref[...].astype(F32), gg[:, :wa]).astype(BF16)
    yb = _rms(yb_ref[...].astype(F32), gg[:, wa:]).astype(BF16)
    y = _dot(ya, w_ref[0, :wa, :]) + _dot(yb, w_ref[0, wa:, :])
    x = x_ref[...] + _rms(y, gpost1_ref[0] * mod[5:6])
    o_ref[...] = _swiglu_step(x, mod, 6, gpre2_ref[0], gpost2_ref[0], wg_ref, wu_ref, wd_ref)


def _outproj_ffn(x, ya, yb, mod_l, gg, gpost1, w_out, gpre2, gpost2, wg, wu, wd, layer, seq,
                 *, tm=512):
    t, d = x.shape
    d_ff = wg.shape[-1]
    wa, wb = ya.shape[1], yb.shape[1]
    tiles_per_batch = seq // tm
    return pl.pallas_call(
        _outproj_ffn_kernel,
        grid=(t // tm,),
        in_specs=[
            pl.BlockSpec((tm, d), lambda i: (i, 0)),
            pl.BlockSpec((tm, wa), lambda i: (i, 0)),
            pl.BlockSpec((tm, wb), lambda i: (i, 0)),
            pl.BlockSpec((1, N_MOD, d), lambda i: (i // tiles_per_batch, 0, 0)),
            _vec_spec(wa + wb), _vec_spec(d),
            pl.BlockSpec((1, wa + wb, d), lambda i: (layer, 0, 0), pipeline_mode=pl.Buffered(1)),
            _vec_spec(d), _vec_spec(d),
            *_ffn_specs(d, d_ff, layer, 1),
        ],
        out_specs=pl.BlockSpec((tm, d), lambda i: (i, 0)),
        out_shape=jax.ShapeDtypeStruct((t, d), F32),
        compiler_params=_cparams(1),
        name="outproj_swiglu",
    )(x, ya, yb, mod_l, gg, gpost1, w_out, gpre2, gpost2, wg, wu, wd)


def _t5_bucket_np(dist):
    n = np.maximum(dist, 0)
    max_exact = T5_BUCKETS // 2
    nf = np.maximum(n, 1).astype(np.float32)
    large = max_exact + (np.log(nf / np.float32(max_exact)) / np.float32(math.log(T5_MAX_DIST / max_exact))
                         * np.float32(T5_BUCKETS - max_exact)).astype(np.int32)
    large = np.minimum(large, T5_BUCKETS - 1)
    return np.where(n < max_exact, n, large).astype(np.int32)


def _bias_kernel(tab_ref, *refs, head_base, heads_per_step):
    n = len(refs) // 2
    step = pl.program_id(0)
    for idx_ref, o_ref in zip(refs[:n], refs[n:]):
        idx = idx_ref[...]
        width = idx.shape[1]
        for hh in range(heads_per_step):
            head = head_base + step * heads_per_step + hh
            out = jnp.full(idx.shape, NEG, F32)
            for bucket in range(T5_BUCKETS):
                out = jnp.where(idx == bucket, tab_ref[bucket, head], out)
            o_ref[0, :, hh * width:(hh + 1) * width] = out


def _bias_expand(rel_bias, buckets, head_base, n_steps, heads_per_step):
    shapes = [b.shape for b in buckets]
    return pl.pallas_call(
        functools.partial(_bias_kernel, head_base=head_base, heads_per_step=heads_per_step),
        grid=(n_steps,),
        in_specs=[pl.BlockSpec(memory_space=pltpu.SMEM)]
        + [pl.BlockSpec(s, lambda h: (0, 0)) for s in shapes],
        out_specs=[pl.BlockSpec((1, s[0], heads_per_step * s[1]), lambda h: (h, 0, 0))
                   for s in shapes],
        out_shape=[jax.ShapeDtypeStruct((n_steps, s[0], heads_per_step * s[1]), F32)
                   for s in shapes],
        compiler_params=_cparams(1),
        name="t5_bias_tables",
    )(rel_bias, *[jnp.asarray(b) for b in buckets])


def _bias_tables(rel_bias, n_swa_heads, n_swa_kv_heads):
    n_heads = rel_bias.shape[1]
    kj = np.arange(2 * SWA_BLOCK)[:, None]
    qi = np.arange(SWA_BLOCK)[None, :]
    dist = qi + SWA_BLOCK - kj
    swa_idx = np.where((dist >= 0) & (dist < SWA_WINDOW), _t5_bucket_np(dist), -1).astype(np.int32)
    (swa,) = _bias_expand(rel_bias, [swa_idx], 0, n_swa_kv_heads,
                          n_swa_heads // n_swa_kv_heads)

    kk = np.arange(MOBA_BLOCK)[:, None]
    qq = np.arange(MOBA_BLOCK)[None, :]
    d_own = qq - kk
    own_idx = np.where(d_own >= 0, _t5_bucket_np(d_own), -1).astype(np.int32)
    prev_idx = _t5_bucket_np(d_own + MOBA_BLOCK)
    n_pairs = (n_heads - n_swa_heads) // 2
    own_t, prev_t = _bias_expand(rel_bias, [own_idx, prev_idx], n_swa_heads, n_pairs, 2)
    far = int(_t5_bucket_np(np.array(MOBA_BLOCK + 1)))
    assert far == int(_t5_bucket_np(np.array(1 << 30)))
    const = jnp.repeat(rel_bias[far, n_swa_heads:], MOBA_BLOCK).reshape(n_pairs, 1, 2 * MOBA_BLOCK)
    return swa, own_t, prev_t, const


def kernel(x, c, rel_bias, ada_w, ada_b, norm_pre, norm_post, ffn_w_gate, ffn_w_up, ffn_w_down,
           mix_w_in, mix_b_in, mix_w_out, attn_sinks, group_gain):
    n_batch, seq, d = x.shape
    depth = ada_w.shape[0]
    n_heads = d // HEAD_DIM
    swa_q_heads = n_heads // 2
    swa_kv_heads = max(1, swa_q_heads // 4)
    moba_heads = n_heads - swa_q_heads
    swa_w = swa_q_heads * HEAD_DIM
    kv_w = swa_kv_heads * HEAD_DIM
    moba_w = moba_heads * HEAD_DIM
    q_col = (swa_w + 2 * kv_w) // LANES
    k_col = q_col + moba_w // LANES
    v_col = k_col + moba_w // LANES

    mod = _modulation(c, ada_w, ada_b).reshape(depth, n_batch, N_MOD, d)
    bias_swa, bias_own_t, bias_prev_t, bias_const = _bias_tables(rel_bias, swa_q_heads, swa_kv_heads)
    sinks = jnp.repeat(attn_sinks, SWA_BLOCK, axis=1).reshape(depth, swa_kv_heads, 1, -1)

    wg = ffn_w_gate.astype(BF16)
    wu = ffn_w_up.astype(BF16)
    wd = ffn_w_down.astype(BF16)
    w_in = mix_w_in.astype(BF16)
    w_out = mix_w_out.astype(BF16)
    b_in = mix_b_in[:, None, :]
    gpre = norm_pre[:, :, None, None, :]
    gpost = norm_post[:, :, None, None, :]
    gg = group_gain[:, None, None, :]

    xt = x.reshape(n_batch * seq, d)
    for l in range(depth):
        xt = _ffn(xt, mod[l], gpre[l, 0], gpost[l, 0], wg, wu, wd, l, seq)
        qkv = _inproj(xt, mod[l], gpre[l, 1], w_in, b_in, l, seq)
        ya = _swa(qkv, bias_swa, sinks[l], n_batch, seq, swa_q_heads, swa_kv_heads)
        yb = _moba(qkv, bias_own_t, bias_prev_t, bias_const, n_batch, seq, q_col, k_col, v_col,
                   moba_heads)
        xt = _outproj_ffn(xt, ya, yb, mod[l], gg[l], gpost[l, 1], w_out, gpre[l, 2], gpost[l, 2],
                          wg, wu, wd, l, seq)
    return xt.reshape(n_batch, seq, d)
```

```python
import functools
import math

import numpy as np
import jax
import jax.numpy as jnp
from jax import lax
from jax.experimental import pallas as pl
from jax.experimental.pallas import tpu as pltpu

HEAD_DIM = 64
LANES = 128
SWA_BLOCK = 128
SWA_WINDOW = 128
MOBA_BLOCK = 256
MOBA_TOPK = 3
FAR_GROUPS = (8, 4, 2, 1)
MOBA_Q_BLOCKS = 2
ONES_ROWS = 16
T5_BUCKETS = 32
T5_MAX_DIST = 128
N_MOD = 9
EPS = 1e-6
NEG = -1e30
VMEM_LIMIT = 56 * 1024 * 1024

F32 = jnp.float32
BF16 = jnp.bfloat16


def _cparams(n_axes):
    return pltpu.CompilerParams(
        dimension_semantics=("arbitrary",) * n_axes,
        vmem_limit_bytes=VMEM_LIMIT,
    )


def _rms(x, g):
    return x * lax.rsqrt(jnp.mean(x * x, axis=-1, keepdims=True) + EPS) * g


def _dot(a, b):
    return jnp.dot(a, b, preferred_element_type=F32)


def _dot_nt(a, b):
    return lax.dot_general(a, b, (((1,), (1,)), ((), ())), preferred_element_type=F32)


def _mod_kernel(ct_ref, w_ref, b_ref, o_ref, *, n_batch, row_chunk):
    ct = ct_ref[...]
    ca = ct * jax.nn.sigmoid(ct)
    d = ct.shape[0]
    for b in range(n_batch):
        acc = b_ref[0]
        for r in range(0, d, row_chunk):
            w = w_ref[0, r:r + row_chunk, :]
            acc = acc + jnp.sum(w * ca[r:r + row_chunk, b:b + 1], axis=0, keepdims=True)
        o_ref[0, b:b + 1, :] = acc


def _modulation(c, ada_w, ada_b, *, tn=1152, row_chunk=128):
    depth, d, n = ada_w.shape
    n_batch = c.shape[0]
    return pl.pallas_call(
        functools.partial(_mod_kernel, n_batch=n_batch, row_chunk=row_chunk),
        grid=(depth, n // tn),
        in_specs=[
            pl.BlockSpec((d, n_batch), lambda l, j: (0, 0)),
            pl.BlockSpec((1, d, tn), lambda l, j: (l, 0, j)),
            pl.BlockSpec((1, 1, tn), lambda l, j: (l, 0, j)),
        ],
        out_specs=pl.BlockSpec((1, n_batch, tn), lambda l, j: (l, 0, j)),
        out_shape=jax.ShapeDtypeStruct((depth, n_batch, n), F32),
        compiler_params=_cparams(2),
        name="adaln_modulation",
    )(c.T, ada_w, ada_b.reshape(depth, 1, n))


def _modulate(x, gpre, mod, base):
    return (_rms(x, gpre * (1.0 + mod[base + 1:base + 2])) + mod[base:base + 1]).astype(BF16)


def _swiglu_step(x, mod, base, gpre, gpost, wg_ref, wu_ref, wd_ref):
    h = _modulate(x, gpre, mod, base)
    g = _dot(h, wg_ref[0, 0])
    u = _dot(h, wu_ref[0, 0])
    y = _dot((g * jax.nn.sigmoid(g) * u).astype(BF16), wd_ref[0, 0])
    return x + _rms(y, gpost * (0.5 * mod[base + 2:base + 3]))


def _ffn_specs(d, d_ff, layer, which):
    const = pl.Buffered(1)
    return [
        pl.BlockSpec((1, 1, d, d_ff), lambda i: (layer, which, 0, 0), pipeline_mode=const),
        pl.BlockSpec((1, 1, d, d_ff), lambda i: (layer, which, 0, 0), pipeline_mode=const),
        pl.BlockSpec((1, 1, d_ff, d), lambda i: (layer, which, 0, 0), pipeline_mode=const),
    ]


def _vec_spec(width):
    return pl.BlockSpec((1, 1, width), lambda i: (0, 0, 0))


def _ffn_kernel(x_ref, mod_ref, gpre_ref, gpost_ref, wg_ref, wu_ref, wd_ref, o_ref):
    o_ref[...] = _swiglu_step(x_ref[...], mod_ref[0], 0, gpre_ref[0], gpost_ref[0],
                              wg_ref, wu_ref, wd_ref)


def _ffn(x, mod_l, gpre, gpost, wg, wu, wd, layer, seq, *, tm=512):
    t, d = x.shape
    tiles_per_batch = seq // tm
    return pl.pallas_call(
        _ffn_kernel,
        grid=(t // tm,),
        in_specs=[
            pl.BlockSpec((tm, d), lambda i: (i, 0)),
            pl.BlockSpec((1, N_MOD, d), lambda i: (i // tiles_per_batch, 0, 0)),
            _vec_spec(d), _vec_spec(d),
            *_ffn_specs(d, wg.shape[-1], layer, 0),
        ],
        out_specs=pl.BlockSpec((tm, d), lambda i: (i, 0)),
        out_shape=jax.ShapeDtypeStruct((t, d), F32),
        compiler_params=_cparams(1),
        name="swiglu_half_step",
    )(x, mod_l, gpre, gpost, wg, wu, wd)


def _inproj_kernel(x_ref, mod_ref, gpre_ref, w_ref, b_ref, o_ref):
    h = _modulate(x_ref[...], gpre_ref[0], mod_ref[0], 3)
    o_ref[...] = (_dot(h, w_ref[0]) + b_ref[0]).astype(o_ref.dtype)


def _inproj(x, mod_l, gpre, w_in, b_in, layer, seq, *, tm=512):
    t, d = x.shape
    n = w_in.shape[-1]
    tiles_per_batch = seq // tm
    return pl.pallas_call(
        _inproj_kernel,
        grid=(t // tm,),
        in_specs=[
            pl.BlockSpec((tm, d), lambda i: (i, 0)),
            pl.BlockSpec((1, N_MOD, d), lambda i: (i // tiles_per_batch, 0, 0)),
            _vec_spec(d),
            pl.BlockSpec((1, d, n), lambda i: (layer, 0, 0), pipeline_mode=pl.Buffered(1)),
            pl.BlockSpec((1, 1, n), lambda i: (layer, 0, 0)),
        ],
        out_specs=pl.BlockSpec((tm, n), lambda i: (i, 0)),
        out_shape=jax.ShapeDtypeStruct((t, n), BF16),
        compiler_params=_cparams(1),
        name="qkv_projection",
    )(x, mod_l, gpre, w_in, b_in)


def _swa_kernel(q_ref, kp_ref, km_ref, vp_ref, vm_ref, bias_ref, sink_ref, o_ref, *, group):
    tq = q_ref.shape[0]
    W = 2 * SWA_BLOCK
    i = pl.program_id(1)
    k_all = jnp.concatenate([kp_ref[...], km_ref[...]], axis=0)
    v_all = jnp.concatenate([vp_ref[...], vm_ref[...]], axis=0)
    vt_all = v_all.astype(F32).T.astype(BF16)
    ones = jnp.ones((ONES_ROWS, v_all.shape[0]), BF16)
    vt = [jnp.concatenate([vt_all[kv * HEAD_DIM:(kv + 1) * HEAD_DIM], ones], axis=0)
          for kv in range(2)]
    feat_lo = lax.broadcasted_iota(jnp.int32, (LANES, SWA_BLOCK), 0) < HEAD_DIM
    key_row = lax.broadcasted_iota(jnp.int32, (W, group * SWA_BLOCK), 0)
    no_prev = jnp.logical_and(key_row < SWA_BLOCK, i == 0)
    scale = HEAD_DIM ** -0.5

    def swap_halves(x):
        return jnp.concatenate([x[HEAD_DIM:], x[:HEAD_DIM]], axis=0)

    for blk in range(tq // SWA_BLOCK):
        r0 = blk * SWA_BLOCK
        kw = k_all[r0:r0 + W]
        for kv in range(2):
            cols = []
            for g in range(group):
                h = kv * group + g
                pair, odd = h // 2, h % 2
                qpt = (q_ref[r0:r0 + SWA_BLOCK, pair * LANES:(pair + 1) * LANES].astype(F32)
                       * scale).T
                if odd != kv:
                    qpt = swap_halves(qpt)
                keep = feat_lo if kv == 0 else jnp.logical_not(feat_lo)
                cols.append(jnp.where(keep, qpt, jnp.zeros_like(qpt)))
            qt = jnp.concatenate(cols, axis=1).astype(BF16)
            st = _dot(kw, qt)
            st = st + bias_ref[kv]
            if blk == 0:
                st = jnp.where(no_prev, NEG, st)
            sink = sink_ref[kv]
            m = jnp.maximum(jnp.max(st, axis=0, keepdims=True), sink)
            acc = _dot(vt[kv][:, r0:r0 + W], jnp.exp(st - m).astype(BF16))
            out_t = acc[:HEAD_DIM] * (1.0 / (acc[HEAD_DIM:HEAD_DIM + 1] + jnp.exp(sink - m)))
            for gp in range(group // 2):
                c0 = 2 * gp * SWA_BLOCK
                pair_t = jnp.concatenate(
                    [out_t[:, c0:c0 + SWA_BLOCK], out_t[:, c0 + SWA_BLOCK:c0 + 2 * SWA_BLOCK]],
                    axis=0)
                pair = (kv * group) // 2 + gp
                o_ref[r0:r0 + SWA_BLOCK, pair * LANES:(pair + 1) * LANES] = pair_t.T.astype(
                    o_ref.dtype)


def _swa(qkv, bias, sinks, n_batch, seq, n_q_heads, n_kv_heads, *, tq=512):
    t = qkv.shape[0]
    nq = seq // tq
    qw = n_q_heads * HEAD_DIM
    k_col = qw // LANES
    v_col = k_col + (n_kv_heads * HEAD_DIM) // LANES
    per = tq // SWA_BLOCK
    group = n_q_heads // n_kv_heads
    assert n_kv_heads * HEAD_DIM == LANES and group % 2 == 0

    def prev_map(col):
        return lambda b, i: (jnp.maximum((b * nq + i) * per - 1, 0), col)

    return pl.pallas_call(
        functools.partial(_swa_kernel, group=group),
        grid=(n_batch, nq),
        in_specs=[
            pl.BlockSpec((tq, qw), lambda b, i: (b * nq + i, 0)),
            pl.BlockSpec((SWA_BLOCK, LANES), prev_map(k_col)),
            pl.BlockSpec((tq, LANES), lambda b, i: (b * nq + i, k_col)),
            pl.BlockSpec((SWA_BLOCK, LANES), prev_map(v_col)),
            pl.BlockSpec((tq, LANES), lambda b, i: (b * nq + i, v_col)),
            pl.BlockSpec((n_kv_heads, 2 * SWA_BLOCK, group * SWA_BLOCK), lambda b, i: (0, 0, 0)),
            pl.BlockSpec((n_kv_heads, 1, group * SWA_BLOCK), lambda b, i: (0, 0, 0)),
        ],
        out_specs=pl.BlockSpec((tq, qw), lambda b, i: (b * nq + i, 0)),
        out_shape=jax.ShapeDtypeStruct((t, qw), BF16),
        compiler_params=_cparams(2),
        name="swa_attention",
    )(qkv, qkv, qkv, qkv, qkv, bias, sinks)


def _moba_kernel(q_ref, k_ref, v_ref, bown_ref, bprev_ref, bconst_ref, o_ref, kmean_ref, vt_ref,
                 st_ref, *, n_blocks):
    L = MOBA_BLOCK
    step = pl.program_id(2)

    @pl.when(step == 0)
    def _():
        kmean_ref[...] = jnp.zeros_like(kmean_ref)
        for jb in range(n_blocks):
            kb = k_ref[jb * L:(jb + 1) * L, :].astype(F32)
            kmean_ref[jb:jb + 1, :] = jnp.sum(kb, axis=0, keepdims=True) * (1.0 / L)
            vt = v_ref[jb * L:(jb + 1) * L, :].astype(F32).T.astype(BF16)
            for head in range(2):
                vt_ref[jb, head, :HEAD_DIM, :] = vt[head * HEAD_DIM:(head + 1) * HEAD_DIM]
                vt_ref[jb, head, HEAD_DIM:, :] = jnp.ones((ONES_ROWS, L), BF16)

    blocks = [_moba_query_block(step * MOBA_Q_BLOCKS + sub, q_ref[sub * L:(sub + 1) * L, :],
                                k_ref, bown_ref, bprev_ref, bconst_ref, kmean_ref, vt_ref,
                                st_ref, n_blocks) for sub in range(MOBA_Q_BLOCKS)]
    next(blocks[0])
    for sub, blk in enumerate(blocks):
        next(blk)
        next(blk)
        if sub + 1 < MOBA_Q_BLOCKS:
            next(blocks[sub + 1])
        o_ref[sub * L:(sub + 1) * L, :] = next(blk).astype(o_ref.dtype)


def _moba_query_block(c, q, k_ref, bown_ref, bprev_ref, bconst_ref, kmean_ref, vt_ref, st_ref,
                      n_blocks):
    L = MOBA_BLOCK
    scale = HEAD_DIM ** -0.5
    qt = (q.astype(F32) * scale).T
    lo = lax.broadcasted_iota(jnp.int32, (LANES, L), 0) < HEAD_DIM
    zero = jnp.zeros_like(qt)
    q2t = jnp.concatenate([jnp.where(lo, qt, zero), jnp.where(lo, zero, qt)],
                          axis=1).astype(BF16)
    cols = 2 * L

    km = kmean_ref[...]
    km_hi = km.astype(BF16)
    km_lo = (km - km_hi.astype(F32)).astype(BF16)
    gate = _dot(km_hi, q2t) + _dot(km_lo, q2t)
    rowf = lax.broadcasted_iota(jnp.int32, gate.shape, 0).astype(F32)
    g = jnp.where(rowf < c.astype(F32), gate, NEG)
    sels = []
    for t in range(MOBA_TOPK):
        gmax = jnp.max(g, axis=0, keepdims=True)
        idx = jnp.min(jnp.where(g == gmax, rowf, float(n_blocks)), axis=0, keepdims=True)
        sels.append(jnp.where(t < c, idx, -1.0))
        g = jnp.where(rowf == idx, -jnp.inf, g)

    def selected(jf):
        hit = sels[0] == jf
        for s_t in sels[1:]:
            hit = jnp.logical_or(hit, s_t == jf)
        return hit

    def scores(j, n=1):
        return _dot(k_ref[pl.ds(pl.multiple_of(j * L, L), n * L), :], q2t)

    def colmax(st):
        return jnp.max(st, axis=0, keepdims=True)

    st_own = scores(c) + bown_ref[0]
    jp = jnp.maximum(c - 1, 0)
    sel_prev = jnp.logical_and(selected((c - 1).astype(F32)), c > 0)
    st_prev = jnp.where(sel_prev, scores(jp) + bprev_ref[0], NEG)
    m_near = jnp.maximum(colmax(st_own), colmax(st_prev))
    yield

    def far_scores(j0, n, m_far):
        st = scores(j0, n)
        for g in range(n):
            stg = jnp.where(selected((j0 + g).astype(F32)), st[g * L:(g + 1) * L], NEG)
            st_ref[j0 + g] = stg
            m_far = jnp.maximum(m_far, colmax(stg))
        return m_far

    n_far = jnp.maximum(c - 1, 0)

    def far_sweep(fn, carry):
        start = jnp.int32(0)
        for n in FAR_GROUPS:
            count = lax.shift_right_logical(n_far - start, int(math.log2(n)))
            carry = lax.fori_loop(0, count, lambda i, cr, s=start, n=n: fn(s + i * n, n, cr),
                                  carry)
            start = start + count * n
        return carry

    m_far = far_sweep(far_scores, jnp.full((1, cols), NEG, F32))
    yield
    bconst = bconst_ref[0]
    m = jnp.maximum(m_near, m_far + bconst)

    def add_pv(acc, j, x):
        p = jnp.exp(x.astype(BF16))
        pv = [_dot(vt_ref[j, head], p[:, head * L:(head + 1) * L]) for head in range(2)]
        return pv if acc is None else [a + b for a, b in zip(acc, pv)]

    acc = add_pv(add_pv(None, c, st_own - m), jp, st_prev - m)
    m_off = m - bconst
    yield

    def far_pv(j0, n, acc):
        for g in range(n):
            acc = add_pv(acc, j0 + g, st_ref[j0 + g] - m_off)
        return acc

    acc = far_sweep(far_pv, acc)
    out_t = jnp.concatenate([a[:HEAD_DIM] * (1.0 / a[HEAD_DIM:HEAD_DIM + 1]) for a in acc],
                            axis=0)
    yield out_t.T


def _moba(qkv, bown_t, bprev_t, bconst, n_batch, seq, q_col, k_col, v_col, n_heads):
    t = qkv.shape[0]
    L = MOBA_BLOCK
    nb = seq // L
    nbp = -(-nb // 8) * 8
    n_pairs = n_heads // 2
    assert nb % MOBA_Q_BLOCKS == 0
    steps = nb // MOBA_Q_BLOCKS
    return pl.pallas_call(
        functools.partial(_moba_kernel, n_blocks=nb),
        grid=(n_batch, n_pairs, steps),
        in_specs=[
            pl.BlockSpec((MOBA_Q_BLOCKS * L, LANES), lambda b, p, c: (b * steps + c, q_col + p)),
            pl.BlockSpec((seq, LANES), lambda b, p, c: (b, k_col + p)),
            pl.BlockSpec((seq, LANES), lambda b, p, c: (b, v_col + p)),
            pl.BlockSpec((1, L, 2 * L), lambda b, p, c: (p, 0, 0)),
            pl.BlockSpec((1, L, 2 * L), lambda b, p, c: (p, 0, 0)),
            pl.BlockSpec((1, 1, 2 * L), lambda b, p, c: (p, 0, 0)),
        ],
        out_specs=pl.BlockSpec((MOBA_Q_BLOCKS * L, LANES), lambda b, p, c: (b * steps + c, p)),
        out_shape=jax.ShapeDtypeStruct((t, n_heads * HEAD_DIM), BF16),
        scratch_shapes=[pltpu.VMEM((nbp, LANES), F32),
                        pltpu.VMEM((nb, 2, HEAD_DIM + ONES_ROWS, L), BF16),
                        pltpu.VMEM((nb, L, 2 * L), F32)],
        compiler_params=_cparams(3),
        name="moba_attention",
    )(qkv, qkv, qkv, bown_t, bprev_t, bconst)


def _outproj_ffn_kernel(x_ref, ya_ref, yb_ref, mod_ref, gg_ref, gpost1_ref, w_ref,
                        gpre2_ref, gpost2_ref, wg_ref, wu_ref, wd_ref, o_ref):
    wa = ya_ref.shape[1]
    mod = mod_ref[0]
    gg = gg_ref[0]
    ya = _rms(ya_ref[...].astype(F32), gg[:, :wa]).astype(BF16)
    yb = _rms(yb_ref[...].astype(F32), gg[:, wa:]).astype(BF16)
    y = _dot(ya, w_ref[0, :wa, :]) + _dot(yb, w_ref[0, wa:, :])
    x = x_ref[...] + _rms(y, gpost1_ref[0] * mod[5:6])
    o_ref[...] = _swiglu_step(x, mod, 6, gpre2_ref[0], gpost2_ref[0], wg_ref, wu_ref, wd_ref)


def _outproj_ffn(x, ya, yb, mod_l, gg, gpost1, w_out, gpre2, gpost2, wg, wu, wd, layer, seq,
                 *, tm=512):
    t, d = x.shape
    d_ff = wg.shape[-1]
    wa, wb = ya.shape[1], yb.shape[1]
    tiles_per_batch = seq // tm
    return pl.pallas_call(
        _outproj_ffn_kernel,
        grid=(t // tm,),
        in_specs=[
            pl.BlockSpec((tm, d), lambda i: (i, 0)),
            pl.BlockSpec((tm, wa), lambda i: (i, 0)),
            pl.BlockSpec((tm, wb), lambda i: (i, 0)),
            pl.BlockSpec((1, N_MOD, d), lambda i: (i // tiles_per_batch, 0, 0)),
            _vec_spec(wa + wb), _vec_spec(d),
            pl.BlockSpec((1, wa + wb, d), lambda i: (layer, 0, 0), pipeline_mode=pl.Buffered(1)),
            _vec_spec(d), _vec_spec(d),
            *_ffn_specs(d, d_ff, layer, 1),
        ],
        out_specs=pl.BlockSpec((tm, d), lambda i: (i, 0)),
        out_shape=jax.ShapeDtypeStruct((t, d), F32),
        compiler_params=_cparams(1),
        name="outproj_swiglu",
    )(x, ya, yb, mod_l, gg, gpost1, w_out, gpre2, gpost2, wg, wu, wd)


def _t5_bucket_np(dist):
    n = np.maximum(dist, 0)
    max_exact = T5_BUCKETS // 2
    nf = np.maximum(n, 1).astype(np.float32)
    large = max_exact + (np.log(nf / np.float32(max_exact)) / np.float32(math.log(T5_MAX_DIST / max_exact))
                         * np.float32(T5_BUCKETS - max_exact)).astype(np.int32)
    large = np.minimum(large, T5_BUCKETS - 1)
    return np.where(n < max_exact, n, large).astype(np.int32)


def _bias_kernel(tab_ref, *refs, head_base, heads_per_step):
    n = len(refs) // 2
    step = pl.program_id(0)
    for idx_ref, o_ref in zip(refs[:n], refs[n:]):
        idx = idx_ref[...]
        width = idx.shape[1]
        for hh in range(heads_per_step):
            head = head_base + step * heads_per_step + hh
            out = jnp.full(idx.shape, NEG, F32)
            for bucket in range(T5_BUCKETS):
                out = jnp.where(idx == bucket, tab_ref[bucket, head], out)
            o_ref[0, :, hh * width:(hh + 1) * width] = out


def _bias_expand(rel_bias, buckets, head_base, n_steps, heads_per_step):
    shapes = [b.shape for b in buckets]
    return pl.pallas_call(
        functools.partial(_bias_kernel, head_base=head_base, heads_per_step=heads_per_step),
        grid=(n_steps,),
        in_specs=[pl.BlockSpec(memory_space=pltpu.SMEM)]
        + [pl.BlockSpec(s, lambda h: (0, 0)) for s in shapes],
        out_specs=[pl.BlockSpec((1, s[0], heads_per_step * s[1]), lambda h: (h, 0, 0))
                   for s in shapes],
        out_shape=[jax.ShapeDtypeStruct((n_steps, s[0], heads_per_step * s[1]), F32)
                   for s in shapes],
        compiler_params=_cparams(1),
        name="t5_bias_tables",
    )(rel_bias, *[jnp.asarray(b) for b in buckets])


def _bias_tables(rel_bias, n_swa_heads, n_swa_kv_heads):
    n_heads = rel_bias.shape[1]
    kj = np.arange(2 * SWA_BLOCK)[:, None]
    qi = np.arange(SWA_BLOCK)[None, :]
    dist = qi + SWA_BLOCK - kj
    swa_idx = np.where((dist >= 0) & (dist < SWA_WINDOW), _t5_bucket_np(dist), -1).astype(np.int32)
    (swa,) = _bias_expand(rel_bias, [swa_idx], 0, n_swa_kv_heads,
                          n_swa_heads // n_swa_kv_heads)

    kk = np.arange(MOBA_BLOCK)[:, None]
    qq = np.arange(MOBA_BLOCK)[None, :]
    d_own = qq - kk
    own_idx = np.where(d_own >= 0, _t5_bucket_np(d_own), -1).astype(np.int32)
    prev_idx = _t5_bucket_np(d_own + MOBA_BLOCK)
    n_pairs = (n_heads - n_swa_heads) // 2
    own_t, prev_t = _bias_expand(rel_bias, [own_idx, prev_idx], n_swa_heads, n_pairs, 2)
    far = int(_t5_bucket_np(np.array(MOBA_BLOCK + 1)))
    assert far == int(_t5_bucket_np(np.array(1 << 30)))
    const = jnp.repeat(rel_bias[far, n_swa_heads:], MOBA_BLOCK).reshape(n_pairs, 1, 2 * MOBA_BLOCK)
    return swa, own_t, prev_t, const


def kernel(x, c, rel_bias, ada_w, ada_b, norm_pre, norm_post, ffn_w_gate, ffn_w_up, ffn_w_down,
           mix_w_in, mix_b_in, mix_w_out, attn_sinks, group_gain):
    n_batch, seq, d = x.shape
    depth = ada_w.shape[0]
    n_heads = d // HEAD_DIM
    swa_q_heads = n_heads // 2
    swa_kv_heads = max(1, swa_q_heads // 4)
    moba_heads = n_heads - swa_q_heads
    swa_w = swa_q_heads * HEAD_DIM
    kv_w = swa_kv_heads * HEAD_DIM
    moba_w = moba_heads * HEAD_DIM
    q_col = (swa_w + 2 * kv_w) // LANES
    k_col = q_col + moba_w // LANES
    v_col = k_col + moba_w // LANES

    mod = _modulation(c, ada_w, ada_b).reshape(depth, n_batch, N_MOD, d)
    bias_swa, bias_own_t, bias_prev_t, bias_const = _bias_tables(rel_bias, swa_q_heads, swa_kv_heads)
    sinks = jnp.repeat(attn_sinks, SWA_BLOCK, axis=1).reshape(depth, swa_kv_heads, 1, -1)

    wg = ffn_w_gate.astype(BF16)
    wu = ffn_w_up.astype(BF16)
    wd = ffn_w_down.astype(BF16)
    w_in = mix_w_in.astype(BF16)
    w_out = mix_w_out.astype(BF16)
    b_in = mix_b_in[:, None, :]
    gpre = norm_pre[:, :, None, None, :]
    gpost = norm_post[:, :, None, None, :]
    gg = group_gain[:, None, None, :]

    xt = x.reshape(n_batch * seq, d)
    for l in range(depth):
        xt = _ffn(xt, mod[l], gpre[l, 0], gpost[l, 0], wg, wu, wd, l, seq)
        qkv = _inproj(xt, mod[l], gpre[l, 1], w_in, b_in, l, seq)
        ya = _swa(qkv, bias_swa, sinks[l], n_batch, seq, swa_q_heads, swa_kv_heads)
        yb = _moba(qkv, bias_own_t, bias_prev_t, bias_const, n_batch, seq, q_col, k_col, v_col,
                   moba_heads)
        xt = _outproj_ffn(xt, ya, yb, mod[l], gg[l], gpost[l, 1], w_out, gpre[l, 2], gpost[l, 2],
                          wg, wu, wd, l, seq)
    return xt.reshape(n_batch, seq, d)
```

```python
import functools
import math

import numpy as np
import jax
import jax.numpy as jnp
from jax import lax
from jax.experimental import pallas as pl
from jax.experimental.pallas import tpu as pltpu

HEAD_DIM = 64
LANES = 128
SWA_BLOCK = 128
SWA_WINDOW = 128
MOBA_BLOCK = 256
MOBA_TOPK = 3
FAR_GROUPS = (8, 4, 2, 1)
MOBA_Q_BLOCKS = 4
CORNER = 128
ONES_ROWS = 16
T5_BUCKETS = 32
T5_MAX_DIST = 128
N_MOD = 9
EPS = 1e-6
NEG = -1e30
VMEM_LIMIT = 56 * 1024 * 1024

F32 = jnp.float32
BF16 = jnp.bfloat16


def _cparams(n_axes):
    return pltpu.CompilerParams(
        dimension_semantics=("arbitrary",) * n_axes,
        vmem_limit_bytes=VMEM_LIMIT,
    )


def _rms(x, g):
    return x * lax.rsqrt(jnp.mean(x * x, axis=-1, keepdims=True) + EPS) * g


def _dot(a, b):
    return jnp.dot(a, b, preferred_element_type=F32)


def _dot_nt(a, b):
    return lax.dot_general(a, b, (((1,), (1,)), ((), ())), preferred_element_type=F32)


def _mod_kernel(ct_ref, w_ref, b_ref, o_ref, *, n_batch, row_chunk):
    ct = ct_ref[...]
    ca = ct * jax.nn.sigmoid(ct)
    d = ct.shape[0]
    for b in range(n_batch):
        acc = b_ref[0]
        for r in range(0, d, row_chunk):
            w = w_ref[0, r:r + row_chunk, :]
            acc = acc + jnp.sum(w * ca[r:r + row_chunk, b:b + 1], axis=0, keepdims=True)
        o_ref[0, b:b + 1, :] = acc


def _modulation(c, ada_w, ada_b, *, tn=1152, row_chunk=128):
    depth, d, n = ada_w.shape
    n_batch = c.shape[0]
    return pl.pallas_call(
        functools.partial(_mod_kernel, n_batch=n_batch, row_chunk=row_chunk),
        grid=(depth, n // tn),
        in_specs=[
            pl.BlockSpec((d, n_batch), lambda l, j: (0, 0)),
            pl.BlockSpec((1, d, tn), lambda l, j: (l, 0, j)),
            pl.BlockSpec((1, 1, tn), lambda l, j: (l, 0, j)),
        ],
        out_specs=pl.BlockSpec((1, n_batch, tn), lambda l, j: (l, 0, j)),
        out_shape=jax.ShapeDtypeStruct((depth, n_batch, n), F32),
        compiler_params=_cparams(2),
        name="adaln_modulation",
    )(c.T, ada_w, ada_b.reshape(depth, 1, n))


def _modulate(x, gpre, mod, base):
    return (_rms(x, gpre * (1.0 + mod[base + 1:base + 2])) + mod[base:base + 1]).astype(BF16)


def _swiglu_step(x, mod, base, gpre, gpost, wg_ref, wu_ref, wd_ref):
    h = _modulate(x, gpre, mod, base)
    g = _dot(h, wg_ref[0, 0])
    u = _dot(h, wu_ref[0, 0])
    y = _dot((g * jax.nn.sigmoid(g) * u).astype(BF16), wd_ref[0, 0])
    return x + _rms(y, gpost * (0.5 * mod[base + 2:base + 3]))


def _ffn_specs(d, d_ff, layer, which):
    const = pl.Buffered(1)
    return [
        pl.BlockSpec((1, 1, d, d_ff), lambda i: (layer, which, 0, 0), pipeline_mode=const),
        pl.BlockSpec((1, 1, d, d_ff), lambda i: (layer, which, 0, 0), pipeline_mode=const),
        pl.BlockSpec((1, 1, d_ff, d), lambda i: (layer, which, 0, 0), pipeline_mode=const),
    ]


def _vec_spec(width):
    return pl.BlockSpec((1, 1, width), lambda i: (0, 0, 0))


def _ffn_kernel(x_ref, mod_ref, gpre_ref, gpost_ref, wg_ref, wu_ref, wd_ref, o_ref):
    o_ref[...] = _swiglu_step(x_ref[...], mod_ref[0], 0, gpre_ref[0], gpost_ref[0],
                              wg_ref, wu_ref, wd_ref)


def _ffn(x, mod_l, gpre, gpost, wg, wu, wd, layer, seq, *, tm=512):
    t, d = x.shape
    tiles_per_batch = seq // tm
    return pl.pallas_call(
        _ffn_kernel,
        grid=(t // tm,),
        in_specs=[
            pl.BlockSpec((tm, d), lambda i: (i, 0)),
            pl.BlockSpec((1, N_MOD, d), lambda i: (i // tiles_per_batch, 0, 0)),
            _vec_spec(d), _vec_spec(d),
            *_ffn_specs(d, wg.shape[-1], layer, 0),
        ],
        out_specs=pl.BlockSpec((tm, d), lambda i: (i, 0)),
        out_shape=jax.ShapeDtypeStruct((t, d), F32),
        compiler_params=_cparams(1),
        name="swiglu_half_step",
    )(x, mod_l, gpre, gpost, wg, wu, wd)


def _inproj_kernel(x_ref, mod_ref, gpre_ref, w_ref, b_ref, o_ref):
    h = _modulate(x_ref[...], gpre_ref[0], mod_ref[0], 3)
    o_ref[...] = (_dot(h, w_ref[0]) + b_ref[0]).astype(o_ref.dtype)


def _inproj(x, mod_l, gpre, w_in, b_in, layer, seq, *, tm=512):
    t, d = x.shape
    n = w_in.shape[-1]
    tiles_per_batch = seq // tm
    return pl.pallas_call(
        _inproj_kernel,
        grid=(t // tm,),
        in_specs=[
            pl.BlockSpec((tm, d), lambda i: (i, 0)),
            pl.BlockSpec((1, N_MOD, d), lambda i: (i // tiles_per_batch, 0, 0)),
            _vec_spec(d),
            pl.BlockSpec((1, d, n), lambda i: (layer, 0, 0), pipeline_mode=pl.Buffered(1)),
            pl.BlockSpec((1, 1, n), lambda i: (layer, 0, 0)),
        ],
        out_specs=pl.BlockSpec((tm, n), lambda i: (i, 0)),
        out_shape=jax.ShapeDtypeStruct((t, n), BF16),
        compiler_params=_cparams(1),
        name="qkv_projection",
    )(x, mod_l, gpre, w_in, b_in)


def _swa_kernel(q_ref, kp_ref, km_ref, vp_ref, vm_ref, bias_ref, sink_ref, o_ref, *, group):
    tq = q_ref.shape[0]
    W = 2 * SWA_BLOCK
    i = pl.program_id(1)
    k_all = jnp.concatenate([kp_ref[...], km_ref[...]], axis=0)
    v_all = jnp.concatenate([vp_ref[...], vm_ref[...]], axis=0)
    vt_all = v_all.astype(F32).T.astype(BF16)
    ones = jnp.ones((ONES_ROWS, v_all.shape[0]), BF16)
    vt = [jnp.concatenate([vt_all[kv * HEAD_DIM:(kv + 1) * HEAD_DIM], ones], axis=0)
          for kv in range(2)]
    feat_lo = lax.broadcasted_iota(jnp.int32, (LANES, SWA_BLOCK), 0) < HEAD_DIM
    key_row = lax.broadcasted_iota(jnp.int32, (W, group * SWA_BLOCK), 0)
    no_prev = jnp.logical_and(key_row < SWA_BLOCK, i == 0)
    scale = HEAD_DIM ** -0.5

    def swap_halves(x):
        return jnp.concatenate([x[HEAD_DIM:], x[:HEAD_DIM]], axis=0)

    for blk in range(tq // SWA_BLOCK):
        r0 = blk * SWA_BLOCK
        kw = k_all[r0:r0 + W]
        for kv in range(2):
            cols = []
            for g in range(group):
                h = kv * group + g
                pair, odd = h // 2, h % 2
                qpt = (q_ref[r0:r0 + SWA_BLOCK, pair * LANES:(pair + 1) * LANES].astype(F32)
                       * scale).T
                if odd != kv:
                    qpt = swap_halves(qpt)
                keep = feat_lo if kv == 0 else jnp.logical_not(feat_lo)
                cols.append(jnp.where(keep, qpt, jnp.zeros_like(qpt)))
            qt = jnp.concatenate(cols, axis=1).astype(BF16)
            st = _dot(kw, qt)
            st = st + bias_ref[kv]
            if blk == 0:
                st = jnp.where(no_prev, NEG, st)
            sink = sink_ref[kv]
            m = jnp.maximum(jnp.max(st, axis=0, keepdims=True), sink)
            acc = _dot(vt[kv][:, r0:r0 + W], jnp.exp(st - m).astype(BF16))
            out_t = acc[:HEAD_DIM] * (1.0 / (acc[HEAD_DIM:HEAD_DIM + 1] + jnp.exp(sink - m)))
            for gp in range(group // 2):
                c0 = 2 * gp * SWA_BLOCK
                pair_t = jnp.concatenate(
                    [out_t[:, c0:c0 + SWA_BLOCK], out_t[:, c0 + SWA_BLOCK:c0 + 2 * SWA_BLOCK]],
                    axis=0)
                pair = (kv * group) // 2 + gp
                o_ref[r0:r0 + SWA_BLOCK, pair * LANES:(pair + 1) * LANES] = pair_t.T.astype(
                    o_ref.dtype)


def _swa(qkv, bias, sinks, n_batch, seq, n_q_heads, n_kv_heads, *, tq=1024):
    t = qkv.shape[0]
    nq = seq // tq
    qw = n_q_heads * HEAD_DIM
    k_col = qw // LANES
    v_col = k_col + (n_kv_heads * HEAD_DIM) // LANES
    per = tq // SWA_BLOCK
    group = n_q_heads // n_kv_heads
    assert n_kv_heads * HEAD_DIM == LANES and group % 2 == 0

    def prev_map(col):
        return lambda b, i: (jnp.maximum((b * nq + i) * per - 1, 0), col)

    return pl.pallas_call(
        functools.partial(_swa_kernel, group=group),
        grid=(n_batch, nq),
        in_specs=[
            pl.BlockSpec((tq, qw), lambda b, i: (b * nq + i, 0)),
            pl.BlockSpec((SWA_BLOCK, LANES), prev_map(k_col)),
            pl.BlockSpec((tq, LANES), lambda b, i: (b * nq + i, k_col)),
            pl.BlockSpec((SWA_BLOCK, LANES), prev_map(v_col)),
            pl.BlockSpec((tq, LANES), lambda b, i: (b * nq + i, v_col)),
            pl.BlockSpec((n_kv_heads, 2 * SWA_BLOCK, group * SWA_BLOCK), lambda b, i: (0, 0, 0)),
            pl.BlockSpec((n_kv_heads, 1, group * SWA_BLOCK), lambda b, i: (0, 0, 0)),
        ],
        out_specs=pl.BlockSpec((tq, qw), lambda b, i: (b * nq + i, 0)),
        out_shape=jax.ShapeDtypeStruct((t, qw), BF16),
        compiler_params=_cparams(2),
        name="swa_attention",
    )(qkv, qkv, qkv, qkv, qkv, bias, sinks)


def _moba_kernel(q_ref, k_ref, v_ref, bown_ref, bcorner_ref, bconst_ref, o_ref, kmean_ref, vt_ref,
                 st_ref, *, n_blocks):
    L = MOBA_BLOCK
    step = pl.program_id(2)

    @pl.when(step == 0)
    def _():
        kmean_ref[...] = jnp.zeros_like(kmean_ref)
        for jb in range(n_blocks):
            kb = k_ref[jb * L:(jb + 1) * L, :].astype(F32)
            kmean_ref[jb:jb + 1, :] = jnp.sum(kb, axis=0, keepdims=True) * (1.0 / L)
            vt = v_ref[jb * L:(jb + 1) * L, :].astype(F32).T.astype(BF16)
            for head in range(2):
                vt_ref[jb, head, :HEAD_DIM, :] = vt[head * HEAD_DIM:(head + 1) * HEAD_DIM]
                vt_ref[jb, head, HEAD_DIM:, :] = jnp.ones((ONES_ROWS, L), BF16)

    blocks = [_moba_query_block(step * MOBA_Q_BLOCKS + sub, q_ref[sub * L:(sub + 1) * L, :],
                                k_ref, bown_ref, bcorner_ref, bconst_ref, kmean_ref, vt_ref,
                                st_ref, n_blocks) for sub in range(MOBA_Q_BLOCKS)]
    next(blocks[0])
    for sub, blk in enumerate(blocks):
        next(blk)
        next(blk)
        if sub + 1 < MOBA_Q_BLOCKS:
            next(blocks[sub + 1])
        o_ref[sub * L:(sub + 1) * L, :] = next(blk).astype(o_ref.dtype)


def _moba_query_block(c, q, k_ref, bown_ref, bcorner_ref, bconst_ref, kmean_ref, vt_ref, st_ref,
                      n_blocks):
    L = MOBA_BLOCK
    scale = HEAD_DIM ** -0.5
    qt = (q.astype(F32) * scale).T
    lo = lax.broadcasted_iota(jnp.int32, (LANES, L), 0) < HEAD_DIM
    zero = jnp.zeros_like(qt)
    q2t = jnp.concatenate([jnp.where(lo, qt, zero), jnp.where(lo, zero, qt)],
                          axis=1).astype(BF16)
    cols = 2 * L

    km = kmean_ref[...]
    km_hi = km.astype(BF16)
    km_lo = (km - km_hi.astype(F32)).astype(BF16)
    gate = _dot(km_hi, q2t) + _dot(km_lo, q2t)
    rowf = lax.broadcasted_iota(jnp.int32, gate.shape, 0).astype(F32)
    g = jnp.where(rowf < c.astype(F32), gate, NEG)
    sels = []
    for t in range(MOBA_TOPK):
        gmax = jnp.max(g, axis=0, keepdims=True)
        idx = jnp.min(jnp.where(g == gmax, rowf, float(n_blocks)), axis=0, keepdims=True)
        sels.append(jnp.where(t < c, idx, -1.0))
        g = jnp.where(rowf == idx, -jnp.inf, g)

    def selected(jf):
        hit = sels[0] == jf
        for s_t in sels[1:]:
            hit = jnp.logical_or(hit, s_t == jf)
        return hit

    def scores(j, n=1):
        return _dot(k_ref[pl.ds(pl.multiple_of(j * L, L), n * L), :], q2t)

    def colmax(st):
        return jnp.max(st, axis=0, keepdims=True)

    st_own = scores(c) + bown_ref[0]
    m_own = colmax(st_own)
    yield

    def far_scores(j0, n, m_far):
        st = scores(j0, n)
        for g in range(n):
            stg = jnp.where(selected((j0 + g).astype(F32)), st[g * L:(g + 1) * L], NEG)
            st_ref[j0 + g] = stg
            m_far = jnp.maximum(m_far, colmax(stg))
        return m_far

    n_far = c

    def far_sweep(fn, carry):
        start = jnp.int32(0)
        for n in FAR_GROUPS:
            count = lax.shift_right_logical(n_far - start, int(math.log2(n)))
            carry = lax.fori_loop(0, count, lambda i, cr, s=start, n=n: fn(s + i * n, n, cr),
                                  carry)
            start = start + count * n
        return carry

    m_far = far_sweep(far_scores, jnp.full((1, cols), NEG, F32))

    def fix_prev(_):
        tops = []
        for head in range(2):
            cs = slice(head * L, head * L + CORNER)
            blk = (st_ref[c - 1, L - CORNER:, cs]
                   + bcorner_ref[0, :, head * CORNER:(head + 1) * CORNER])
            st_ref[c - 1, L - CORNER:, cs] = blk
            tops += [colmax(blk), jnp.full((1, L - CORNER), NEG, F32)]
        return jnp.concatenate(tops, axis=1)

    m_corner = lax.cond(c > 0, fix_prev, lambda _: jnp.full((1, cols), NEG, F32), 0)
    yield
    bconst = bconst_ref[0]
    m = jnp.maximum(m_own, jnp.maximum(m_far, m_corner) + bconst)

    def add_pv(acc, j, x):
        p = jnp.exp(x.astype(BF16))
        pv = [_dot(vt_ref[j, head], p[:, head * L:(head + 1) * L]) for head in range(2)]
        return pv if acc is None else [a + b for a, b in zip(acc, pv)]

    acc = add_pv(None, c, st_own - m)
    m_off = m - bconst
    yield

    def far_pv(j0, n, acc):
        for g in range(n):
            acc = add_pv(acc, j0 + g, st_ref[j0 + g] - m_off)
        return acc

    acc = far_sweep(far_pv, acc)
    out_t = jnp.concatenate([a[:HEAD_DIM] * (1.0 / a[HEAD_DIM:HEAD_DIM + 1]) for a in acc],
                            axis=0)
    yield out_t.T


def _moba(qkv, bown_t, bcorner_t, bconst, n_batch, seq, q_col, k_col, v_col, n_heads):
    t = qkv.shape[0]
    L = MOBA_BLOCK
    nb = seq // L
    nbp = -(-nb // 8) * 8
    n_pairs = n_heads // 2
    assert nb % MOBA_Q_BLOCKS == 0
    steps = nb // MOBA_Q_BLOCKS
    return pl.pallas_call(
        functools.partial(_moba_kernel, n_blocks=nb),
        grid=(n_batch, n_pairs, steps),
        in_specs=[
            pl.BlockSpec((MOBA_Q_BLOCKS * L, LANES), lambda b, p, c: (b * steps + c, q_col + p)),
            pl.BlockSpec((seq, LANES), lambda b, p, c: (b, k_col + p)),
            pl.BlockSpec((seq, LANES), lambda b, p, c: (b, v_col + p)),
            pl.BlockSpec((1, L, 2 * L), lambda b, p, c: (p, 0, 0)),
            pl.BlockSpec((1, CORNER, 2 * CORNER), lambda b, p, c: (p, 0, 0)),
            pl.BlockSpec((1, 1, 2 * L), lambda b, p, c: (p, 0, 0)),
        ],
        out_specs=pl.BlockSpec((MOBA_Q_BLOCKS * L, LANES), lambda b, p, c: (b * steps + c, p)),
        out_shape=jax.ShapeDtypeStruct((t, n_heads * HEAD_DIM), BF16),
        scratch_shapes=[pltpu.VMEM((nbp, LANES), F32),
                        pltpu.VMEM((nb, 2, HEAD_DIM + ONES_ROWS, L), BF16),
                        pltpu.VMEM((nb, L, 2 * L), F32)],
        compiler_params=_cparams(3),
        name="moba_attention",
    )(qkv, qkv, qkv, bown_t, bcorner_t, bconst)


def _outproj_ffn_kernel(x_ref, ya_ref, yb_ref, mod_ref, gg_ref, gpost1_ref, w_ref,
                        gpre2_ref, gpost2_ref, wg_ref, wu_ref, wd_ref, o_ref):
    wa = ya_ref.shape[1]
    mod = mod_ref[0]
    gg = gg_ref[0]
    ya = _rms(ya_ref[...].astype(F32), gg[:, :wa]).astype(BF16)
    yb = _rms(yb_ref[...].astype(F32), gg[:, wa:]).astype(BF16)
    y = _dot(ya, w_ref[0, :wa, :]) + _dot(yb, w_ref[0, wa:, :])
    x = x_ref[...] + _rms(y, gpost1_ref[0] * mod[5:6])
    o_ref[...] = _swiglu_step(x, mod, 6, gpre2_ref[0], gpost2_ref[0], wg_ref, wu_ref, wd_ref)


def _outproj_ffn(x, ya, yb, mod_l, gg, gpost1, w_out, gpre2, gpost2, wg, wu, wd, layer, seq,
                 *, tm=512):
    t, d = x.shape
    d_ff = wg.shape[-1]
    wa, wb = ya.shape[1], yb.shape[1]
    tiles_per_batch = seq // tm
    return pl.pallas_call(
        _outproj_ffn_kernel,
        grid=(t // tm,),
        in_specs=[
            pl.BlockSpec((tm, d), lambda i: (i, 0)),
            pl.BlockSpec((tm, wa), lambda i: (i, 0)),
            pl.BlockSpec((tm, wb), lambda i: (i, 0)),
            pl.BlockSpec((1, N_MOD, d), lambda i: (i // tiles_per_batch, 0, 0)),
            _vec_spec(wa + wb), _vec_spec(d),
            pl.BlockSpec((1, wa + wb, d), lambda i: (layer, 0, 0), pipeline_mode=pl.Buffered(1)),
            _vec_spec(d), _vec_spec(d),
            *_ffn_specs(d, d_ff, layer, 1),
        ],
        out_specs=pl.BlockSpec((tm, d), lambda i: (i, 0)),
        out_shape=jax.ShapeDtypeStruct((t, d), F32),
        compiler_params=_cparams(1),
        name="outproj_swiglu",
    )(x, ya, yb, mod_l, gg, gpost1, w_out, gpre2, gpost2, wg, wu, wd)


def _t5_bucket_np(dist):
    n = np.maximum(dist, 0)
    max_exact = T5_BUCKETS // 2
    nf = np.maximum(n, 1).astype(np.float32)
    large = max_exact + (np.log(nf / np.float32(max_exact)) / np.float32(math.log(T5_MAX_DIST / max_exact))
                         * np.float32(T5_BUCKETS - max_exact)).astype(np.int32)
    large = np.minimum(large, T5_BUCKETS - 1)
    return np.where(n < max_exact, n, large).astype(np.int32)


def _bias_kernel(tab_ref, *refs, head_base, heads_per_step):
    n = len(refs) // 2
    step = pl.program_id(0)
    for idx_ref, o_ref in zip(refs[:n], refs[n:]):
        idx = idx_ref[...]
        width = idx.shape[1]
        for hh in range(heads_per_step):
            head = head_base + step * heads_per_step + hh
            out = jnp.full(idx.shape, NEG, F32)
            for bucket in range(T5_BUCKETS):
                out = jnp.where(idx == bucket, tab_ref[bucket, head], out)
            o_ref[0, :, hh * width:(hh + 1) * width] = out


def _bias_expand(rel_bias, buckets, head_base, n_steps, heads_per_step):
    shapes = [b.shape for b in buckets]
    return pl.pallas_call(
        functools.partial(_bias_kernel, head_base=head_base, heads_per_step=heads_per_step),
        grid=(n_steps,),
        in_specs=[pl.BlockSpec(memory_space=pltpu.SMEM)]
        + [pl.BlockSpec(s, lambda h: (0, 0)) for s in shapes],
        out_specs=[pl.BlockSpec((1, s[0], heads_per_step * s[1]), lambda h: (h, 0, 0))
                   for s in shapes],
        out_shape=[jax.ShapeDtypeStruct((n_steps, s[0], heads_per_step * s[1]), F32)
                   for s in shapes],
        compiler_params=_cparams(1),
        name="t5_bias_tables",
    )(rel_bias, *[jnp.asarray(b) for b in buckets])


def _bias_tables(rel_bias, n_swa_heads, n_swa_kv_heads):
    n_heads = rel_bias.shape[1]
    kj = np.arange(2 * SWA_BLOCK)[:, None]
    qi = np.arange(SWA_BLOCK)[None, :]
    dist = qi + SWA_BLOCK - kj
    swa_idx = np.where((dist >= 0) & (dist < SWA_WINDOW), _t5_bucket_np(dist), -1).astype(np.int32)
    (swa,) = _bias_expand(rel_bias, [swa_idx], 0, n_swa_kv_heads,
                          n_swa_heads // n_swa_kv_heads)

    kk = np.arange(MOBA_BLOCK)[:, None]
    qq = np.arange(MOBA_BLOCK)[None, :]
    d_own = qq - kk
    own_idx = np.where(d_own >= 0, _t5_bucket_np(d_own), -1).astype(np.int32)
    far = int(_t5_bucket_np(np.array(1 << 30)))
    prev_idx = _t5_bucket_np(d_own + MOBA_BLOCK)
    outside = np.ones_like(prev_idx, dtype=bool)
    outside[MOBA_BLOCK - CORNER:, :CORNER] = False
    assert np.all(prev_idx[outside] == far)
    corner_idx = prev_idx[MOBA_BLOCK - CORNER:, :CORNER]
    n_pairs = (n_heads - n_swa_heads) // 2
    own_t, corner_t = _bias_expand(rel_bias, [own_idx, corner_idx], n_swa_heads, n_pairs, 2)
    const = rel_bias[far, n_swa_heads:]
    corner_t = corner_t - jnp.repeat(const, CORNER).reshape(n_pairs, 1, 2 * CORNER)
    const = jnp.repeat(const, MOBA_BLOCK).reshape(n_pairs, 1, 2 * MOBA_BLOCK)
    return swa, own_t, corner_t, const


def kernel(x, c, rel_bias, ada_w, ada_b, norm_pre, norm_post, ffn_w_gate, ffn_w_up, ffn_w_down,
           mix_w_in, mix_b_in, mix_w_out, attn_sinks, group_gain):
    n_batch, seq, d = x.shape
    depth = ada_w.shape[0]
    n_heads = d // HEAD_DIM
    swa_q_heads = n_heads // 2
    swa_kv_heads = max(1, swa_q_heads // 4)
    moba_heads = n_heads - swa_q_heads
    swa_w = swa_q_heads * HEAD_DIM
    kv_w = swa_kv_heads * HEAD_DIM
    moba_w = moba_heads * HEAD_DIM
    q_col = (swa_w + 2 * kv_w) // LANES
    k_col = q_col + moba_w // LANES
    v_col = k_col + moba_w // LANES

    mod = _modulation(c, ada_w, ada_b).reshape(depth, n_batch, N_MOD, d)
    bias_swa, bias_own_t, bias_corner_t, bias_const = _bias_tables(rel_bias, swa_q_heads, swa_kv_heads)
    sinks = jnp.repeat(attn_sinks, SWA_BLOCK, axis=1).reshape(depth, swa_kv_heads, 1, -1)

    wg = ffn_w_gate.astype(BF16)
    wu = ffn_w_up.astype(BF16)
    wd = ffn_w_down.astype(BF16)
    w_in = mix_w_in.astype(BF16)
    w_out = mix_w_out.astype(BF16)
    b_in = mix_b_in[:, None, :]
    gpre = norm_pre[:, :, None, None, :]
    gpost = norm_post[:, :, None, None, :]
    gg = group_gain[:, None, None, :]

    xt = x.reshape(n_batch * seq, d)
    for l in range(depth):
        xt = _ffn(xt, mod[l], gpre[l, 0], gpost[l, 0], wg, wu, wd, l, seq)
        qkv = _inproj(xt, mod[l], gpre[l, 1], w_in, b_in, l, seq)
        ya = _swa(qkv, bias_swa, sinks[l], n_batch, seq, swa_q_heads, swa_kv_heads)
        yb = _moba(qkv, bias_own_t, bias_corner_t, bias_const, n_batch, seq, q_col, k_col, v_col,
                   moba_heads)
        xt = _outproj_ffn(xt, ya, yb, mod[l], gg[l], gpost[l, 1], w_out, gpre[l, 2], gpost[l, 2],
                          wg, wu, wd, l, seq)
    return xt.reshape(n_batch, seq, d)
```

```python
import functools
import math

import numpy as np
import jax
import jax.numpy as jnp
from jax import lax
from jax.experimental import pallas as pl
from jax.experimental.pallas import tpu as pltpu

HEAD_DIM = 64
LANES = 128
SWA_BLOCK = 128
SWA_WINDOW = 128
MOBA_BLOCK = 256
MOBA_TOPK = 3
FAR_GROUPS = (8, 4, 2, 1)
MOBA_Q_BLOCKS = 4
CORNER = 128
BF16_SUBLANES = 16
ONES_ROWS = BF16_SUBLANES
T5_BUCKETS = 32
T5_MAX_DIST = 128
N_MOD = 9
EPS = 1e-6
NEG = -1e30
VMEM_LIMIT = 56 * 1024 * 1024

F32 = jnp.float32
BF16 = jnp.bfloat16


def _cparams(n_axes):
    return pltpu.CompilerParams(
        dimension_semantics=("arbitrary",) * n_axes,
        vmem_limit_bytes=VMEM_LIMIT,
    )


def _rms(x, g):
    return x * lax.rsqrt(jnp.mean(x * x, axis=-1, keepdims=True) + EPS) * g


def _dot(a, b):
    return jnp.dot(a, b, preferred_element_type=F32)


def _dot_nt(a, b):
    return lax.dot_general(a, b, (((1,), (1,)), ((), ())), preferred_element_type=F32)


def _mod_kernel(ct_ref, w_ref, b_ref, o_ref, *, n_batch, row_chunk):
    ct = ct_ref[...]
    ca = ct * jax.nn.sigmoid(ct)
    d = ct.shape[0]
    for b in range(n_batch):
        acc = b_ref[0]
        for r in range(0, d, row_chunk):
            w = w_ref[0, r:r + row_chunk, :]
            acc = acc + jnp.sum(w * ca[r:r + row_chunk, b:b + 1], axis=0, keepdims=True)
        o_ref[0, b:b + 1, :] = acc


def _modulation(c, ada_w, ada_b, *, tn=1152, row_chunk=128):
    depth, d, n = ada_w.shape
    n_batch = c.shape[0]
    return pl.pallas_call(
        functools.partial(_mod_kernel, n_batch=n_batch, row_chunk=row_chunk),
        grid=(depth, n // tn),
        in_specs=[
            pl.BlockSpec((d, n_batch), lambda l, j: (0, 0)),
            pl.BlockSpec((1, d, tn), lambda l, j: (l, 0, j)),
            pl.BlockSpec((1, 1, tn), lambda l, j: (l, 0, j)),
        ],
        out_specs=pl.BlockSpec((1, n_batch, tn), lambda l, j: (l, 0, j)),
        out_shape=jax.ShapeDtypeStruct((depth, n_batch, n), F32),
        compiler_params=_cparams(2),
        name="adaln_modulation",
    )(c.T, ada_w, ada_b.reshape(depth, 1, n))


def _modulate(x, gpre, mod, base):
    return (_rms(x, gpre * (1.0 + mod[base + 1:base + 2])) + mod[base:base + 1]).astype(BF16)


def _swiglu_step(x, mod, base, gpre, gpost, w_refs):
    wg_ref, wu_ref, wd_ref = w_refs
    h = _modulate(x, gpre, mod, base)
    g = _dot(h, wg_ref[...])
    u = _dot(h, wu_ref[...])
    y = _dot((g * jax.nn.sigmoid(g) * u).astype(BF16), wd_ref[...])
    return x + _rms(y, gpost * (0.5 * mod[base + 2:base + 3]))


def _ffn_specs(d, d_ff):
    const = pl.Buffered(1)
    return [
        pl.BlockSpec((d, d_ff), lambda i: (0, 0), pipeline_mode=const),
        pl.BlockSpec((d, d_ff), lambda i: (0, 0), pipeline_mode=const),
        pl.BlockSpec((d_ff, d), lambda i: (0, 0), pipeline_mode=const),
    ]


def _vec_spec(width):
    return pl.BlockSpec((1, 1, width), lambda i: (0, 0, 0))


def _slab_rows(n_rows, n_steps):
    return min(r for r in range(BF16_SUBLANES, n_rows + 1, BF16_SUBLANES)
               if n_rows % r == 0 and r * n_steps >= n_rows)


def _cast_plan(ffn_w, layer, which, n_steps):
    in_specs, out_specs, out_shape = [], [], []
    for w in ffn_w:
        n_rows, n_cols = w.shape[-2:]
        rows = _slab_rows(n_rows, n_steps)

        def slab(i, last=n_rows // rows - 1):
            return jnp.minimum(i, last)

        in_specs.append(pl.BlockSpec((1, 1, rows, n_cols),
                                     lambda i, slab=slab: (layer, which, slab(i), 0)))
        out_specs.append(pl.BlockSpec((rows, n_cols), lambda i, slab=slab: (slab(i), 0)))
        out_shape.append(jax.ShapeDtypeStruct((n_rows, n_cols), BF16))
    return in_specs, out_specs, out_shape


def _cast_slabs(src_refs, dst_refs):
    for src, dst in zip(src_refs, dst_refs):
        dst[...] = src[0, 0].astype(dst.dtype)


def _ffn_kernel(*refs, n_cast):
    x_ref, mod_ref, gpre_ref, gpost_ref = refs[:4]
    w_refs, src_refs = refs[4:7], refs[7:7 + n_cast]
    o_ref, dst_refs = refs[7 + n_cast], refs[8 + n_cast:]
    o_ref[...] = _swiglu_step(x_ref[...], mod_ref[0], 0, gpre_ref[0], gpost_ref[0], w_refs)
    _cast_slabs(src_refs, dst_refs)


def _ffn(x, mod_l, gpre, gpost, w_bf16, ffn_w, layer, seq, *, tm=512):
    t, d = x.shape
    d_ff = w_bf16[0].shape[-1]
    tiles_per_batch = seq // tm
    n_steps = t // tm
    cast_in, cast_out, cast_shape = _cast_plan(ffn_w, layer, 1, n_steps)
    out = pl.pallas_call(
        functools.partial(_ffn_kernel, n_cast=len(cast_in)),
        grid=(n_steps,),
        in_specs=[
            pl.BlockSpec((tm, d), lambda i: (i, 0)),
            pl.BlockSpec((1, N_MOD, d), lambda i: (i // tiles_per_batch, 0, 0)),
            _vec_spec(d), _vec_spec(d),
            *_ffn_specs(d, d_ff),
            *cast_in,
        ],
        out_specs=[pl.BlockSpec((tm, d), lambda i: (i, 0)), *cast_out],
        out_shape=[jax.ShapeDtypeStruct((t, d), F32), *cast_shape],
        compiler_params=_cparams(1),
        name="swiglu_half_step",
    )(x, mod_l, gpre, gpost, *w_bf16, *ffn_w)
    return out[0], tuple(out[1:])


def _inproj_kernel(x_ref, mod_ref, gpre_ref, w_ref, b_ref, o_ref):
    h = _modulate(x_ref[...], gpre_ref[0], mod_ref[0], 3)
    o_ref[...] = (_dot(h, w_ref[0]) + b_ref[0]).astype(o_ref.dtype)


def _inproj(x, mod_l, gpre, w_in, b_in, layer, seq, *, tm=512):
    t, d = x.shape
    n = w_in.shape[-1]
    tiles_per_batch = seq // tm
    return pl.pallas_call(
        _inproj_kernel,
        grid=(t // tm,),
        in_specs=[
            pl.BlockSpec((tm, d), lambda i: (i, 0)),
            pl.BlockSpec((1, N_MOD, d), lambda i: (i // tiles_per_batch, 0, 0)),
            _vec_spec(d),
            pl.BlockSpec((1, d, n), lambda i: (layer, 0, 0), pipeline_mode=pl.Buffered(1)),
            pl.BlockSpec((1, 1, n), lambda i: (layer, 0, 0)),
        ],
        out_specs=pl.BlockSpec((tm, n), lambda i: (i, 0)),
        out_shape=jax.ShapeDtypeStruct((t, n), BF16),
        compiler_params=_cparams(1),
        name="qkv_projection",
    )(x, mod_l, gpre, w_in, b_in)


def _swa_kernel(q_ref, kp_ref, km_ref, vp_ref, vm_ref, bias_ref, sink_ref, o_ref, *, group):
    tq = q_ref.shape[0]
    W = 2 * SWA_BLOCK
    i = pl.program_id(1)
    k_all = jnp.concatenate([kp_ref[...], km_ref[...]], axis=0)
    v_all = jnp.concatenate([vp_ref[...], vm_ref[...]], axis=0)
    vt_all = v_all.astype(F32).T.astype(BF16)
    ones = jnp.ones((ONES_ROWS, v_all.shape[0]), BF16)
    vt = [jnp.concatenate([vt_all[kv * HEAD_DIM:(kv + 1) * HEAD_DIM], ones], axis=0)
          for kv in range(2)]
    feat_lo = lax.broadcasted_iota(jnp.int32, (LANES, SWA_BLOCK), 0) < HEAD_DIM
    key_row = lax.broadcasted_iota(jnp.int32, (W, group * SWA_BLOCK), 0)
    no_prev = jnp.logical_and(key_row < SWA_BLOCK, i == 0)
    scale = HEAD_DIM ** -0.5

    def swap_halves(x):
        return jnp.concatenate([x[HEAD_DIM:], x[:HEAD_DIM]], axis=0)

    for blk in range(tq // SWA_BLOCK):
        r0 = blk * SWA_BLOCK
        kw = k_all[r0:r0 + W]
        for kv in range(2):
            cols = []
            for g in range(group):
                h = kv * group + g
                pair, odd = h // 2, h % 2
                qpt = (q_ref[r0:r0 + SWA_BLOCK, pair * LANES:(pair + 1) * LANES].astype(F32)
                       * scale).T
                if odd != kv:
                    qpt = swap_halves(qpt)
                keep = feat_lo if kv == 0 else jnp.logical_not(feat_lo)
                cols.append(jnp.where(keep, qpt, jnp.zeros_like(qpt)))
            qt = jnp.concatenate(cols, axis=1).astype(BF16)
            st = _dot(kw, qt)
            st = st + bias_ref[kv]
            if blk == 0:
                st = jnp.where(no_prev, NEG, st)
            sink = sink_ref[kv]
            m = jnp.maximum(jnp.max(st, axis=0, keepdims=True), sink)
            acc = _dot(vt[kv][:, r0:r0 + W], jnp.exp(st - m).astype(BF16))
            out_t = acc[:HEAD_DIM] * (1.0 / (acc[HEAD_DIM:HEAD_DIM + 1] + jnp.exp(sink - m)))
            for gp in range(group // 2):
                c0 = 2 * gp * SWA_BLOCK
                pair_t = jnp.concatenate(
                    [out_t[:, c0:c0 + SWA_BLOCK], out_t[:, c0 + SWA_BLOCK:c0 + 2 * SWA_BLOCK]],
                    axis=0)
                pair = (kv * group) // 2 + gp
                o_ref[r0:r0 + SWA_BLOCK, pair * LANES:(pair + 1) * LANES] = pair_t.T.astype(
                    o_ref.dtype)


def _swa(qkv, bias, sinks, n_batch, seq, n_q_heads, n_kv_heads, *, tq=1024):
    t = qkv.shape[0]
    nq = seq // tq
    qw = n_q_heads * HEAD_DIM
    k_col = qw // LANES
    v_col = k_col + (n_kv_heads * HEAD_DIM) // LANES
    per = tq // SWA_BLOCK
    group = n_q_heads // n_kv_heads
    assert n_kv_heads * HEAD_DIM == LANES and group % 2 == 0

    def prev_map(col):
        return lambda b, i: (jnp.maximum((b * nq + i) * per - 1, 0), col)

    return pl.pallas_call(
        functools.partial(_swa_kernel, group=group),
        grid=(n_batch, nq),
        in_specs=[
            pl.BlockSpec((tq, qw), lambda b, i: (b * nq + i, 0)),
            pl.BlockSpec((SWA_BLOCK, LANES), prev_map(k_col)),
            pl.BlockSpec((tq, LANES), lambda b, i: (b * nq + i, k_col)),
            pl.BlockSpec((SWA_BLOCK, LANES), prev_map(v_col)),
            pl.BlockSpec((tq, LANES), lambda b, i: (b * nq + i, v_col)),
            pl.BlockSpec((n_kv_heads, 2 * SWA_BLOCK, group * SWA_BLOCK), lambda b, i: (0, 0, 0)),
            pl.BlockSpec((n_kv_heads, 1, group * SWA_BLOCK), lambda b, i: (0, 0, 0)),
        ],
        out_specs=pl.BlockSpec((tq, qw), lambda b, i: (b * nq + i, 0)),
        out_shape=jax.ShapeDtypeStruct((t, qw), BF16),
        compiler_params=_cparams(2),
        name="swa_attention",
    )(qkv, qkv, qkv, qkv, qkv, bias, sinks)


def _moba_kernel(q_ref, k_ref, v_ref, bown_ref, bcorner_ref, bconst_ref, o_ref, kmean_ref, vt_ref,
                 st_ref, *, n_blocks):
    L = MOBA_BLOCK
    step = pl.program_id(2)

    @pl.when(step == 0)
    def _():
        kmean_ref[...] = jnp.zeros_like(kmean_ref)
        for jb in range(n_blocks):
            kb = k_ref[jb * L:(jb + 1) * L, :].astype(F32)
            kmean_ref[jb:jb + 1, :] = jnp.sum(kb, axis=0, keepdims=True) * (1.0 / L)
            vt = v_ref[jb * L:(jb + 1) * L, :].astype(F32).T.astype(BF16)
            for head in range(2):
                vt_ref[jb, head, :HEAD_DIM, :] = vt[head * HEAD_DIM:(head + 1) * HEAD_DIM]
                vt_ref[jb, head, HEAD_DIM:, :] = jnp.ones((ONES_ROWS, L), BF16)

    blocks = [_moba_query_block(step * MOBA_Q_BLOCKS + sub, q_ref[sub * L:(sub + 1) * L, :],
                                k_ref, bown_ref, bcorner_ref, bconst_ref, kmean_ref, vt_ref,
                                st_ref, n_blocks) for sub in range(MOBA_Q_BLOCKS)]
    next(blocks[0])
    for sub, blk in enumerate(blocks):
        next(blk)
        next(blk)
        if sub + 1 < MOBA_Q_BLOCKS:
            next(blocks[sub + 1])
        o_ref[sub * L:(sub + 1) * L, :] = next(blk).astype(o_ref.dtype)


def _moba_query_block(c, q, k_ref, bown_ref, bcorner_ref, bconst_ref, kmean_ref, vt_ref, st_ref,
                      n_blocks):
    L = MOBA_BLOCK
    scale = HEAD_DIM ** -0.5
    qt = (q.astype(F32) * scale).T
    lo = lax.broadcasted_iota(jnp.int32, (LANES, L), 0) < HEAD_DIM
    zero = jnp.zeros_like(qt)
    q2t = jnp.concatenate([jnp.where(lo, qt, zero), jnp.where(lo, zero, qt)],
                          axis=1).astype(BF16)
    cols = 2 * L

    km = kmean_ref[...]
    km_hi = km.astype(BF16)
    km_lo = (km - km_hi.astype(F32)).astype(BF16)
    gate = _dot(km_hi, q2t) + _dot(km_lo, q2t)
    rowf = lax.broadcasted_iota(jnp.int32, gate.shape, 0).astype(F32)
    g = jnp.where(rowf < c.astype(F32), gate, NEG)
    sels = []
    for t in range(MOBA_TOPK):
        gmax = jnp.max(g, axis=0, keepdims=True)
        idx = jnp.min(jnp.where(g == gmax, rowf, float(n_blocks)), axis=0, keepdims=True)
        sels.append(jnp.where(t < c, idx, -1.0))
        g = jnp.where(rowf == idx, -jnp.inf, g)

    def selected(jf):
        hit = sels[0] == jf
        for s_t in sels[1:]:
            hit = jnp.logical_or(hit, s_t == jf)
        return hit

    def scores(j, n=1):
        return _dot(k_ref[pl.ds(pl.multiple_of(j * L, L), n * L), :], q2t)

    def colmax(st):
        return jnp.max(st, axis=0, keepdims=True)

    st_own = scores(c) + bown_ref[0]
    m_own = colmax(st_own)
    yield

    def far_scores(j0, n, m_far):
        st = scores(j0, n)
        for g in range(n):
            stg = jnp.where(selected((j0 + g).astype(F32)), st[g * L:(g + 1) * L], NEG)
            st_ref[j0 + g] = stg
            m_far = jnp.maximum(m_far, colmax(stg))
        return m_far

    n_far = c

    def far_sweep(fn, carry):
        start = jnp.int32(0)
        for n in FAR_GROUPS:
            count = lax.shift_right_logical(n_far - start, int(math.log2(n)))
            carry = lax.fori_loop(0, count, lambda i, cr, s=start, n=n: fn(s + i * n, n, cr),
                                  carry)
            start = start + count * n
        return carry

    m_far = far_sweep(far_scores, jnp.full((1, cols), NEG, F32))

    def fix_prev(_):
        tops = []
        for head in range(2):
            cs = slice(head * L, head * L + CORNER)
            blk = (st_ref[c - 1, L - CORNER:, cs]
                   + bcorner_ref[0, :, head * CORNER:(head + 1) * CORNER])
            st_ref[c - 1, L - CORNER:, cs] = blk
            tops += [colmax(blk), jnp.full((1, L - CORNER), NEG, F32)]
        return jnp.concatenate(tops, axis=1)

    m_corner = lax.cond(c > 0, fix_prev, lambda _: jnp.full((1, cols), NEG, F32), 0)
    yield
    bconst = bconst_ref[0]
    m = jnp.maximum(m_own, jnp.maximum(m_far, m_corner) + bconst)

    def add_pv(acc, j, x):
        p = jnp.exp(x.astype(BF16))
        pv = [_dot(vt_ref[j, head], p[:, head * L:(head + 1) * L]) for head in range(2)]
        return pv if acc is None else [a + b for a, b in zip(acc, pv)]

    acc = add_pv(None, c, st_own - m)
    m_off = m - bconst
    yield

    def far_pv(j0, n, acc):
        for g in range(n):
            acc = add_pv(acc, j0 + g, st_ref[j0 + g] - m_off)
        return acc

    acc = far_sweep(far_pv, acc)
    out_t = jnp.concatenate([a[:HEAD_DIM] * (1.0 / a[HEAD_DIM:HEAD_DIM + 1]) for a in acc],
                            axis=0)
    yield out_t.T


def _moba(qkv, bown_t, bcorner_t, bconst, n_batch, seq, q_col, k_col, v_col, n_heads):
    t = qkv.shape[0]
    L = MOBA_BLOCK
    nb = seq // L
    nbp = -(-nb // 8) * 8
    n_pairs = n_heads // 2
    assert nb % MOBA_Q_BLOCKS == 0
    steps = nb // MOBA_Q_BLOCKS
    return pl.pallas_call(
        functools.partial(_moba_kernel, n_blocks=nb),
        grid=(n_batch, n_pairs, steps),
        in_specs=[
            pl.BlockSpec((MOBA_Q_BLOCKS * L, LANES), lambda b, p, c: (b * steps + c, q_col + p)),
            pl.BlockSpec((seq, LANES), lambda b, p, c: (b, k_col + p)),
            pl.BlockSpec((seq, LANES), lambda b, p, c: (b, v_col + p)),
            pl.BlockSpec((1, L, 2 * L), lambda b, p, c: (p, 0, 0)),
            pl.BlockSpec((1, CORNER, 2 * CORNER), lambda b, p, c: (p, 0, 0)),
            pl.BlockSpec((1, 1, 2 * L), lambda b, p, c: (p, 0, 0)),
        ],
        out_specs=pl.BlockSpec((MOBA_Q_BLOCKS * L, LANES), lambda b, p, c: (b * steps + c, p)),
        out_shape=jax.ShapeDtypeStruct((t, n_heads * HEAD_DIM), BF16),
        scratch_shapes=[pltpu.VMEM((nbp, LANES), F32),
                        pltpu.VMEM((nb, 2, HEAD_DIM + ONES_ROWS, L), BF16),
                        pltpu.VMEM((nb, L, 2 * L), F32)],
        compiler_params=_cparams(3),
        name="moba_attention",
    )(qkv, qkv, qkv, bown_t, bcorner_t, bconst)


def _outproj_ffn_kernel(*refs, n_cast):
    x_ref, ya_ref, yb_ref, mod_ref, gg_ref, gpost1_ref, w_ref, gpre2_ref, gpost2_ref = refs[:9]
    w_refs, src_refs = refs[9:12], refs[12:12 + n_cast]
    o_ref, dst_refs = refs[12 + n_cast], refs[13 + n_cast:]
    wa = ya_ref.shape[1]
    mod = mod_ref[0]
    gg = gg_ref[0]
    ya = _rms(ya_ref[...].astype(F32), gg[:, :wa]).astype(BF16)
    yb = _rms(yb_ref[...].astype(F32), gg[:, wa:]).astype(BF16)
    y = _dot(ya, w_ref[0, :wa, :]) + _dot(yb, w_ref[0, wa:, :])
    x = x_ref[...] + _rms(y, gpost1_ref[0] * mod[5:6])
    o_ref[...] = _swiglu_step(x, mod, 6, gpre2_ref[0], gpost2_ref[0], w_refs)
    _cast_slabs(src_refs, dst_refs)


def _outproj_ffn(x, ya, yb, mod_l, gg, gpost1, w_out, gpre2, gpost2, w_bf16, ffn_w, layer,
                 cast_next, seq, *, tm=512):
    t, d = x.shape
    d_ff = w_bf16[0].shape[-1]
    wa, wb = ya.shape[1], yb.shape[1]
    tiles_per_batch = seq // tm
    n_steps = t // tm
    cast_in, cast_out, cast_shape = (_cast_plan(ffn_w, layer + 1, 0, n_steps) if cast_next
                                     else ([], [], []))
    out = pl.pallas_call(
        functools.partial(_outproj_ffn_kernel, n_cast=len(cast_in)),
        grid=(n_steps,),
        in_specs=[
            pl.BlockSpec((tm, d), lambda i: (i, 0)),
            pl.BlockSpec((tm, wa), lambda i: (i, 0)),
            pl.BlockSpec((tm, wb), lambda i: (i, 0)),
            pl.BlockSpec((1, N_MOD, d), lambda i: (i // tiles_per_batch, 0, 0)),
            _vec_spec(wa + wb), _vec_spec(d),
            pl.BlockSpec((1, wa + wb, d), lambda i: (layer, 0, 0), pipeline_mode=pl.Buffered(1)),
            _vec_spec(d), _vec_spec(d),
            *_ffn_specs(d, d_ff),
            *cast_in,
        ],
        out_specs=[pl.BlockSpec((tm, d), lambda i: (i, 0)), *cast_out],
        out_shape=[jax.ShapeDtypeStruct((t, d), F32), *cast_shape],
        compiler_params=_cparams(1),
        name="outproj_swiglu",
    )(x, ya, yb, mod_l, gg, gpost1, w_out, gpre2, gpost2, *w_bf16, *(ffn_w if cast_next else ()))
    return out[0], (tuple(out[1:]) if cast_next else None)


def _t5_bucket_np(dist):
    n = np.maximum(dist, 0)
    max_exact = T5_BUCKETS // 2
    nf = np.maximum(n, 1).astype(np.float32)
    large = max_exact + (np.log(nf / np.float32(max_exact)) / np.float32(math.log(T5_MAX_DIST / max_exact))
                         * np.float32(T5_BUCKETS - max_exact)).astype(np.int32)
    large = np.minimum(large, T5_BUCKETS - 1)
    return np.where(n < max_exact, n, large).astype(np.int32)


def _bias_kernel(tab_ref, *refs, head_base, heads_per_step):
    n = len(refs) // 2
    step = pl.program_id(0)
    for idx_ref, o_ref in zip(refs[:n], refs[n:]):
        idx = idx_ref[...]
        width = idx.shape[1]
        for hh in range(heads_per_step):
            head = head_base + step * heads_per_step + hh
            out = jnp.full(idx.shape, NEG, F32)
            for bucket in range(T5_BUCKETS):
                out = jnp.where(idx == bucket, tab_ref[bucket, head], out)
            o_ref[0, :, hh * width:(hh + 1) * width] = out


def _bias_expand(rel_bias, buckets, head_base, n_steps, heads_per_step):
    shapes = [b.shape for b in buckets]
    return pl.pallas_call(
        functools.partial(_bias_kernel, head_base=head_base, heads_per_step=heads_per_step),
        grid=(n_steps,),
        in_specs=[pl.BlockSpec(memory_space=pltpu.SMEM)]
        + [pl.BlockSpec(s, lambda h: (0, 0)) for s in shapes],
        out_specs=[pl.BlockSpec((1, s[0], heads_per_step * s[1]), lambda h: (h, 0, 0))
                   for s in shapes],
        out_shape=[jax.ShapeDtypeStruct((n_steps, s[0], heads_per_step * s[1]), F32)
                   for s in shapes],
        compiler_params=_cparams(1),
        name="t5_bias_tables",
    )(rel_bias, *[jnp.asarray(b) for b in buckets])


def _bias_tables(rel_bias, n_swa_heads, n_swa_kv_heads):
    n_heads = rel_bias.shape[1]
    kj = np.arange(2 * SWA_BLOCK)[:, None]
    qi = np.arange(SWA_BLOCK)[None, :]
    dist = qi + SWA_BLOCK - kj
    swa_idx = np.where((dist >= 0) & (dist < SWA_WINDOW), _t5_bucket_np(dist), -1).astype(np.int32)
    (swa,) = _bias_expand(rel_bias, [swa_idx], 0, n_swa_kv_heads,
                          n_swa_heads // n_swa_kv_heads)

    kk = np.arange(MOBA_BLOCK)[:, None]
    qq = np.arange(MOBA_BLOCK)[None, :]
    d_own = qq - kk
    own_idx = np.where(d_own >= 0, _t5_bucket_np(d_own), -1).astype(np.int32)
    far = int(_t5_bucket_np(np.array(1 << 30)))
    prev_idx = _t5_bucket_np(d_own + MOBA_BLOCK)
    outside = np.ones_like(prev_idx, dtype=bool)
    outside[MOBA_BLOCK - CORNER:, :CORNER] = False
    assert np.all(prev_idx[outside] == far)
    corner_idx = prev_idx[MOBA_BLOCK - CORNER:, :CORNER]
    n_pairs = (n_heads - n_swa_heads) // 2
    own_t, corner_t = _bias_expand(rel_bias, [own_idx, corner_idx], n_swa_heads, n_pairs, 2)
    const = rel_bias[far, n_swa_heads:]
    corner_t = corner_t - jnp.repeat(const, CORNER).reshape(n_pairs, 1, 2 * CORNER)
    const = jnp.repeat(const, MOBA_BLOCK).reshape(n_pairs, 1, 2 * MOBA_BLOCK)
    return swa, own_t, corner_t, const


def kernel(x, c, rel_bias, ada_w, ada_b, norm_pre, norm_post, ffn_w_gate, ffn_w_up, ffn_w_down,
           mix_w_in, mix_b_in, mix_w_out, attn_sinks, group_gain):
    n_batch, seq, d = x.shape
    depth = ada_w.shape[0]
    n_heads = d // HEAD_DIM
    swa_q_heads = n_heads // 2
    swa_kv_heads = max(1, swa_q_heads // 4)
    moba_heads = n_heads - swa_q_heads
    swa_w = swa_q_heads * HEAD_DIM
    kv_w = swa_kv_heads * HEAD_DIM
    moba_w = moba_heads * HEAD_DIM
    q_col = (swa_w + 2 * kv_w) // LANES
    k_col = q_col + moba_w // LANES
    v_col = k_col + moba_w // LANES

    mod = _modulation(c, ada_w, ada_b).reshape(depth, n_batch, N_MOD, d)
    bias_swa, bias_own_t, bias_corner_t, bias_const = _bias_tables(rel_bias, swa_q_heads, swa_kv_heads)
    sinks = jnp.repeat(attn_sinks, SWA_BLOCK, axis=1).reshape(depth, swa_kv_heads, 1, -1)

    ffn_w = (ffn_w_gate, ffn_w_up, ffn_w_down)
    w_first = tuple(w[0, 0].astype(BF16) for w in ffn_w)
    w_in = mix_w_in.astype(BF16)
    w_out = mix_w_out.astype(BF16)
    b_in = mix_b_in[:, None, :]
    gpre = norm_pre[:, :, None, None, :]
    gpost = norm_post[:, :, None, None, :]
    gg = group_gain[:, None, None, :]

    xt = x.reshape(n_batch * seq, d)
    for l in range(depth):
        xt, w_second = _ffn(xt, mod[l], gpre[l, 0], gpost[l, 0], w_first, ffn_w, l, seq)
        qkv = _inproj(xt, mod[l], gpre[l, 1], w_in, b_in, l, seq)
        ya = _swa(qkv, bias_swa, sinks[l], n_batch, seq, swa_q_heads, swa_kv_heads)
        yb = _moba(qkv, bias_own_t, bias_corner_t, bias_const, n_batch, seq, q_col, k_col, v_col,
                   moba_heads)
        xt, w_first = _outproj_ffn(xt, ya, yb, mod[l], gg[l], gpost[l, 1], w_out, gpre[l, 2],
                                   gpost[l, 2], w_second, ffn_w, l, l + 1 < depth, seq)
    return xt.reshape(n_batch, seq, d)
```

```python
import functools
import math

import numpy as np
import jax
import jax.numpy as jnp
from jax import lax
from jax.experimental import pallas as pl
from jax.experimental.pallas import tpu as pltpu

HEAD_DIM = 64
LANES = 128
SWA_BLOCK = 128
SWA_WINDOW = 128
MOBA_BLOCK = 256
MOBA_TOPK = 3
FAR_GROUPS = (8, 4, 2, 1)
MOBA_Q_BLOCKS = 4
CORNER = 128
BF16_SUBLANES = 16
ONES_ROWS = BF16_SUBLANES
T5_BUCKETS = 32
T5_MAX_DIST = 128
N_MOD = 9
EPS = 1e-6
NEG = -1e30
VMEM_LIMIT = 56 * 1024 * 1024

F32 = jnp.float32
BF16 = jnp.bfloat16


def _cparams(n_axes):
    return pltpu.CompilerParams(
        dimension_semantics=("arbitrary",) * n_axes,
        vmem_limit_bytes=VMEM_LIMIT,
    )


def _rms(x, g):
    return x * lax.rsqrt(jnp.mean(x * x, axis=-1, keepdims=True) + EPS) * g


def _dot(a, b):
    return jnp.dot(a, b, preferred_element_type=F32)


def _dot_nt(a, b):
    return lax.dot_general(a, b, (((1,), (1,)), ((), ())), preferred_element_type=F32)


def _mod_kernel(ct_ref, w_ref, b_ref, o_ref, *, n_batch, row_chunk):
    ct = ct_ref[...]
    ca = ct * jax.nn.sigmoid(ct)
    d = ct.shape[0]
    for b in range(n_batch):
        acc = b_ref[0]
        for r in range(0, d, row_chunk):
            w = w_ref[0, r:r + row_chunk, :]
            acc = acc + jnp.sum(w * ca[r:r + row_chunk, b:b + 1], axis=0, keepdims=True)
        o_ref[0, b:b + 1, :] = acc


def _modulation(c, ada_w, ada_b, *, tn=1152, row_chunk=128):
    depth, d, n = ada_w.shape
    n_batch = c.shape[0]
    return pl.pallas_call(
        functools.partial(_mod_kernel, n_batch=n_batch, row_chunk=row_chunk),
        grid=(depth, n // tn),
        in_specs=[
            pl.BlockSpec((d, n_batch), lambda l, j: (0, 0)),
            pl.BlockSpec((1, d, tn), lambda l, j: (l, 0, j)),
            pl.BlockSpec((1, 1, tn), lambda l, j: (l, 0, j)),
        ],
        out_specs=pl.BlockSpec((1, n_batch, tn), lambda l, j: (l, 0, j)),
        out_shape=jax.ShapeDtypeStruct((depth, n_batch, n), F32),
        compiler_params=_cparams(2),
        name="adaln_modulation",
    )(c.T, ada_w, ada_b.reshape(depth, 1, n))


def _modulate(x, gpre, mod, base):
    return (_rms(x, gpre * (1.0 + mod[base + 1:base + 2])) + mod[base:base + 1]).astype(BF16)


def _swiglu_step(x, mod, base, gpre, gpost, w_refs):
    wg_ref, wu_ref, wd_ref = w_refs
    h = _modulate(x, gpre, mod, base)
    g = _dot(h, wg_ref[...])
    u = _dot(h, wu_ref[...])
    y = _dot((g * jax.nn.sigmoid(g) * u).astype(BF16), wd_ref[...])
    return x + _rms(y, gpost * (0.5 * mod[base + 2:base + 3]))


def _ffn_specs(d, d_ff):
    const = pl.Buffered(1)
    return [
        pl.BlockSpec((d, d_ff), lambda i: (0, 0), pipeline_mode=const),
        pl.BlockSpec((d, d_ff), lambda i: (0, 0), pipeline_mode=const),
        pl.BlockSpec((d_ff, d), lambda i: (0, 0), pipeline_mode=const),
    ]


def _vec_spec(width):
    return pl.BlockSpec((1, 1, width), lambda i: (0, 0, 0))


def _slab_rows(n_rows, n_steps):
    return min(r for r in range(BF16_SUBLANES, n_rows + 1, BF16_SUBLANES)
               if n_rows % r == 0 and r * n_steps >= n_rows)


def _cast_plan(sources, n_steps):
    in_specs, out_specs, out_shape = [], [], []
    for w, lead in sources:
        n_rows, n_cols = w.shape[-2:]
        rows = _slab_rows(n_rows, n_steps)

        def slab(i, last=n_rows // rows - 1):
            return jnp.minimum(i, last)

        in_specs.append(pl.BlockSpec((1,) * len(lead) + (rows, n_cols),
                                     lambda i, slab=slab, lead=lead: (*lead, slab(i), 0)))
        out_specs.append(pl.BlockSpec((rows, n_cols), lambda i, slab=slab: (slab(i), 0)))
        out_shape.append(jax.ShapeDtypeStruct((n_rows, n_cols), BF16))
    return in_specs, out_specs, out_shape


def _cast_slabs(src_refs, dst_refs):
    for src, dst in zip(src_refs, dst_refs):
        dst[...] = src[...].reshape(dst.shape).astype(dst.dtype)


def _ffn_kernel(*refs, n_cast):
    x_ref, mod_ref, gpre_ref, gpost_ref = refs[:4]
    w_refs, src_refs = refs[4:7], refs[7:7 + n_cast]
    o_ref, dst_refs = refs[7 + n_cast], refs[8 + n_cast:]
    o_ref[...] = _swiglu_step(x_ref[...], mod_ref[0], 0, gpre_ref[0], gpost_ref[0], w_refs)
    _cast_slabs(src_refs, dst_refs)


def _ffn(x, mod_l, gpre, gpost, w_bf16, cast_sources, seq, *, tm=512):
    t, d = x.shape
    d_ff = w_bf16[0].shape[-1]
    tiles_per_batch = seq // tm
    n_steps = t // tm
    cast_in, cast_out, cast_shape = _cast_plan(cast_sources, n_steps)
    out = pl.pallas_call(
        functools.partial(_ffn_kernel, n_cast=len(cast_in)),
        grid=(n_steps,),
        in_specs=[
            pl.BlockSpec((tm, d), lambda i: (i, 0)),
            pl.BlockSpec((1, N_MOD, d), lambda i: (i // tiles_per_batch, 0, 0)),
            _vec_spec(d), _vec_spec(d),
            *_ffn_specs(d, d_ff),
            *cast_in,
        ],
        out_specs=[pl.BlockSpec((tm, d), lambda i: (i, 0)), *cast_out],
        out_shape=[jax.ShapeDtypeStruct((t, d), F32), *cast_shape],
        compiler_params=_cparams(1),
        name="swiglu_half_step",
    )(x, mod_l, gpre, gpost, *w_bf16, *[w for w, _ in cast_sources])
    return out[0], tuple(out[1:])


def _inproj_kernel(x_ref, mod_ref, gpre_ref, w_ref, b_ref, o_ref):
    h = _modulate(x_ref[...], gpre_ref[0], mod_ref[0], 3)
    o_ref[...] = (_dot(h, w_ref[...]) + b_ref[0]).astype(o_ref.dtype)


def _inproj(x, mod_l, gpre, w_in, b_in, layer, seq, *, tm=512):
    t, d = x.shape
    n = w_in.shape[-1]
    tiles_per_batch = seq // tm
    return pl.pallas_call(
        _inproj_kernel,
        grid=(t // tm,),
        in_specs=[
            pl.BlockSpec((tm, d), lambda i: (i, 0)),
            pl.BlockSpec((1, N_MOD, d), lambda i: (i // tiles_per_batch, 0, 0)),
            _vec_spec(d),
            pl.BlockSpec((d, n), lambda i: (0, 0), pipeline_mode=pl.Buffered(1)),
            pl.BlockSpec((1, 1, n), lambda i: (layer, 0, 0)),
        ],
        out_specs=pl.BlockSpec((tm, n), lambda i: (i, 0)),
        out_shape=jax.ShapeDtypeStruct((t, n), BF16),
        compiler_params=_cparams(1),
        name="qkv_projection",
    )(x, mod_l, gpre, w_in, b_in)


def _swa_kernel(q_ref, kp_ref, km_ref, vp_ref, vm_ref, bias_ref, sink_ref, o_ref, *, group):
    tq = q_ref.shape[0]
    W = 2 * SWA_BLOCK
    i = pl.program_id(1)
    k_all = jnp.concatenate([kp_ref[...], km_ref[...]], axis=0)
    v_all = jnp.concatenate([vp_ref[...], vm_ref[...]], axis=0)
    vt_all = v_all.astype(F32).T.astype(BF16)
    ones = jnp.ones((ONES_ROWS, v_all.shape[0]), BF16)
    vt = [jnp.concatenate([vt_all[kv * HEAD_DIM:(kv + 1) * HEAD_DIM], ones], axis=0)
          for kv in range(2)]
    feat_lo = lax.broadcasted_iota(jnp.int32, (LANES, SWA_BLOCK), 0) < HEAD_DIM
    key_row = lax.broadcasted_iota(jnp.int32, (W, group * SWA_BLOCK), 0)
    no_prev = jnp.logical_and(key_row < SWA_BLOCK, i == 0)
    scale = HEAD_DIM ** -0.5

    def swap_halves(x):
        return jnp.concatenate([x[HEAD_DIM:], x[:HEAD_DIM]], axis=0)

    for blk in range(tq // SWA_BLOCK):
        r0 = blk * SWA_BLOCK
        kw = k_all[r0:r0 + W]
        for kv in range(2):
            cols = []
            for g in range(group):
                h = kv * group + g
                pair, odd = h // 2, h % 2
                qpt = (q_ref[r0:r0 + SWA_BLOCK, pair * LANES:(pair + 1) * LANES].astype(F32)
                       * scale).T
                if odd != kv:
                    qpt = swap_halves(qpt)
                keep = feat_lo if kv == 0 else jnp.logical_not(feat_lo)
                cols.append(jnp.where(keep, qpt, jnp.zeros_like(qpt)))
            qt = jnp.concatenate(cols, axis=1).astype(BF16)
            st = _dot(kw, qt)
            st = st + bias_ref[kv]
            if blk == 0:
                st = jnp.where(no_prev, NEG, st)
            sink = sink_ref[kv]
            m = jnp.maximum(jnp.max(st, axis=0, keepdims=True), sink)
            acc = _dot(vt[kv][:, r0:r0 + W], jnp.exp(st - m).astype(BF16))
            out_t = acc[:HEAD_DIM] * (1.0 / (acc[HEAD_DIM:HEAD_DIM + 1] + jnp.exp(sink - m)))
            for gp in range(group // 2):
                c0 = 2 * gp * SWA_BLOCK
                pair_t = jnp.concatenate(
                    [out_t[:, c0:c0 + SWA_BLOCK], out_t[:, c0 + SWA_BLOCK:c0 + 2 * SWA_BLOCK]],
                    axis=0)
                pair = (kv * group) // 2 + gp
                o_ref[r0:r0 + SWA_BLOCK, pair * LANES:(pair + 1) * LANES] = pair_t.T.astype(
                    o_ref.dtype)


def _swa(qkv, bias, sinks, n_batch, seq, n_q_heads, n_kv_heads, *, tq=1024):
    t = qkv.shape[0]
    nq = seq // tq
    qw = n_q_heads * HEAD_DIM
    k_col = qw // LANES
    v_col = k_col + (n_kv_heads * HEAD_DIM) // LANES
    per = tq // SWA_BLOCK
    group = n_q_heads // n_kv_heads
    assert n_kv_heads * HEAD_DIM == LANES and group % 2 == 0

    def prev_map(col):
        return lambda b, i: (jnp.maximum((b * nq + i) * per - 1, 0), col)

    return pl.pallas_call(
        functools.partial(_swa_kernel, group=group),
        grid=(n_batch, nq),
        in_specs=[
            pl.BlockSpec((tq, qw), lambda b, i: (b * nq + i, 0)),
            pl.BlockSpec((SWA_BLOCK, LANES), prev_map(k_col)),
            pl.BlockSpec((tq, LANES), lambda b, i: (b * nq + i, k_col)),
            pl.BlockSpec((SWA_BLOCK, LANES), prev_map(v_col)),
            pl.BlockSpec((tq, LANES), lambda b, i: (b * nq + i, v_col)),
            pl.BlockSpec((n_kv_heads, 2 * SWA_BLOCK, group * SWA_BLOCK), lambda b, i: (0, 0, 0)),
            pl.BlockSpec((n_kv_heads, 1, group * SWA_BLOCK), lambda b, i: (0, 0, 0)),
        ],
        out_specs=pl.BlockSpec((tq, qw), lambda b, i: (b * nq + i, 0)),
        out_shape=jax.ShapeDtypeStruct((t, qw), BF16),
        compiler_params=_cparams(2),
        name="swa_attention",
    )(qkv, qkv, qkv, qkv, qkv, bias, sinks)


def _moba_kernel(q_ref, k_ref, v_ref, bown_ref, bcorner_ref, bconst_ref, o_ref, kmean_ref, vt_ref,
                 st_ref, *, n_blocks):
    L = MOBA_BLOCK
    step = pl.program_id(2)

    @pl.when(step == 0)
    def _():
        kmean_ref[...] = jnp.zeros_like(kmean_ref)
        for jb in range(n_blocks):
            kb = k_ref[jb * L:(jb + 1) * L, :].astype(F32)
            kmean_ref[jb:jb + 1, :] = jnp.sum(kb, axis=0, keepdims=True) * (1.0 / L)
            vt = v_ref[jb * L:(jb + 1) * L, :].astype(F32).T.astype(BF16)
            for head in range(2):
                vt_ref[jb, head, :HEAD_DIM, :] = vt[head * HEAD_DIM:(head + 1) * HEAD_DIM]
                vt_ref[jb, head, HEAD_DIM:, :] = jnp.ones((ONES_ROWS, L), BF16)

    blocks = [_moba_query_block(step * MOBA_Q_BLOCKS + sub, q_ref[sub * L:(sub + 1) * L, :],
                                k_ref, bown_ref, bcorner_ref, bconst_ref, kmean_ref, vt_ref,
                                st_ref, n_blocks) for sub in range(MOBA_Q_BLOCKS)]
    for blk in blocks:
        next(blk)
    for sub, blk in enumerate(blocks):
        next(blk)
        next(blk)
        o_ref[sub * L:(sub + 1) * L, :] = next(blk).astype(o_ref.dtype)


def _moba_query_block(c, q, k_ref, bown_ref, bcorner_ref, bconst_ref, kmean_ref, vt_ref, st_ref,
                      n_blocks):
    L = MOBA_BLOCK
    scale = HEAD_DIM ** -0.5
    qt = (q.astype(F32) * scale).T
    lo = lax.broadcasted_iota(jnp.int32, (LANES, L), 0) < HEAD_DIM
    zero = jnp.zeros_like(qt)
    q2t = jnp.concatenate([jnp.where(lo, qt, zero), jnp.where(lo, zero, qt)],
                          axis=1).astype(BF16)
    cols = 2 * L

    km = kmean_ref[...]
    km_hi = km.astype(BF16)
    km_lo = (km - km_hi.astype(F32)).astype(BF16)
    gate = _dot(km_hi, q2t) + _dot(km_lo, q2t)
    rowf = lax.broadcasted_iota(jnp.int32, gate.shape, 0).astype(F32)
    g = jnp.where(rowf < c.astype(F32), gate, NEG)
    sels = []
    for t in range(MOBA_TOPK):
        gmax = jnp.max(g, axis=0, keepdims=True)
        idx = jnp.min(jnp.where(g == gmax, rowf, float(n_blocks)), axis=0, keepdims=True)
        sels.append(jnp.where(t < c, idx, -1.0))
        g = jnp.where(rowf == idx, -jnp.inf, g)

    def selected(jf):
        hit = sels[0] == jf
        for s_t in sels[1:]:
            hit = jnp.logical_or(hit, s_t == jf)
        return hit

    def scores(j, n=1):
        return _dot(k_ref[pl.ds(pl.multiple_of(j * L, L), n * L), :], q2t)

    def colmax(st):
        return jnp.max(st, axis=0, keepdims=True)

    st_own = scores(c) + bown_ref[0]
    m_own = colmax(st_own)
    yield

    def far_scores(j0, n, m_far):
        st = scores(j0, n)
        for g in range(n):
            stg = jnp.where(selected((j0 + g).astype(F32)), st[g * L:(g + 1) * L], NEG)
            st_ref[j0 + g] = stg
            m_far = jnp.maximum(m_far, colmax(stg))
        return m_far

    n_far = c

    def far_sweep(fn, carry):
        start = jnp.int32(0)
        for n in FAR_GROUPS:
            count = lax.shift_right_logical(n_far - start, int(math.log2(n)))
            carry = lax.fori_loop(0, count, lambda i, cr, s=start, n=n: fn(s + i * n, n, cr),
                                  carry)
            start = start + count * n
        return carry

    m_far = far_sweep(far_scores, jnp.full((1, cols), NEG, F32))

    def fix_prev(_):
        tops = []
        for head in range(2):
            cs = slice(head * L, head * L + CORNER)
            blk = (st_ref[c - 1, L - CORNER:, cs]
                   + bcorner_ref[0, :, head * CORNER:(head + 1) * CORNER])
            st_ref[c - 1, L - CORNER:, cs] = blk
            tops += [colmax(blk), jnp.full((1, L - CORNER), NEG, F32)]
        return jnp.concatenate(tops, axis=1)

    m_corner = lax.cond(c > 0, fix_prev, lambda _: jnp.full((1, cols), NEG, F32), 0)
    yield
    bconst = bconst_ref[0]
    m = jnp.maximum(m_own, jnp.maximum(m_far, m_corner) + bconst)

    def add_pv(acc, j, x):
        p = jnp.exp(x.astype(BF16))
        pv = [_dot(vt_ref[j, head], p[:, head * L:(head + 1) * L]) for head in range(2)]
        return pv if acc is None else [a + b for a, b in zip(acc, pv)]

    acc = add_pv(None, c, st_own - m)
    m_off = m - bconst
    yield

    def far_pv(j0, n, acc):
        for g in range(n):
            acc = add_pv(acc, j0 + g, st_ref[j0 + g] - m_off)
        return acc

    acc = far_sweep(far_pv, acc)
    out_t = jnp.concatenate([a[:HEAD_DIM] * (1.0 / a[HEAD_DIM:HEAD_DIM + 1]) for a in acc],
                            axis=0)
    yield out_t.T


def _moba(qkv, bown_t, bcorner_t, bconst, n_batch, seq, q_col, k_col, v_col, n_heads):
    t = qkv.shape[0]
    L = MOBA_BLOCK
    nb = seq // L
    nbp = -(-nb // 8) * 8
    n_pairs = n_heads // 2
    assert nb % MOBA_Q_BLOCKS == 0
    steps = nb // MOBA_Q_BLOCKS
    return pl.pallas_call(
        functools.partial(_moba_kernel, n_blocks=nb),
        grid=(n_batch, n_pairs, steps),
        in_specs=[
            pl.BlockSpec((MOBA_Q_BLOCKS * L, LANES), lambda b, p, c: (b * steps + c, q_col + p)),
            pl.BlockSpec((seq, LANES), lambda b, p, c: (b, k_col + p)),
            pl.BlockSpec((seq, LANES), lambda b, p, c: (b, v_col + p)),
            pl.BlockSpec((1, L, 2 * L), lambda b, p, c: (p, 0, 0)),
            pl.BlockSpec((1, CORNER, 2 * CORNER), lambda b, p, c: (p, 0, 0)),
            pl.BlockSpec((1, 1, 2 * L), lambda b, p, c: (p, 0, 0)),
        ],
        out_specs=pl.BlockSpec((MOBA_Q_BLOCKS * L, LANES), lambda b, p, c: (b * steps + c, p)),
        out_shape=jax.ShapeDtypeStruct((t, n_heads * HEAD_DIM), BF16),
        scratch_shapes=[pltpu.VMEM((nbp, LANES), F32),
                        pltpu.VMEM((nb, 2, HEAD_DIM + ONES_ROWS, L), BF16),
                        pltpu.VMEM((nb, L, 2 * L), F32)],
        compiler_params=_cparams(3),
        name="moba_attention",
    )(qkv, qkv, qkv, bown_t, bcorner_t, bconst)


def _outproj_ffn_kernel(*refs, n_cast):
    x_ref, ya_ref, yb_ref, mod_ref, gg_ref, gpost1_ref, w_ref, gpre2_ref, gpost2_ref = refs[:9]
    w_refs, src_refs = refs[9:12], refs[12:12 + n_cast]
    o_ref, dst_refs = refs[12 + n_cast], refs[13 + n_cast:]
    wa = ya_ref.shape[1]
    mod = mod_ref[0]
    gg = gg_ref[0]
    ya = _rms(ya_ref[...].astype(F32), gg[:, :wa]).astype(BF16)
    yb = _rms(yb_ref[...].astype(F32), gg[:, wa:]).astype(BF16)
    y = _dot(ya, w_ref[:wa, :]) + _dot(yb, w_ref[wa:, :])
    x = x_ref[...] + _rms(y, gpost1_ref[0] * mod[5:6])
    o_ref[...] = _swiglu_step(x, mod, 6, gpre2_ref[0], gpost2_ref[0], w_refs)
    _cast_slabs(src_refs, dst_refs)


def _outproj_ffn(x, ya, yb, mod_l, gg, gpost1, w_out, gpre2, gpost2, w_bf16, cast_sources, seq,
                 *, tm=512):
    t, d = x.shape
    d_ff = w_bf16[0].shape[-1]
    wa, wb = ya.shape[1], yb.shape[1]
    tiles_per_batch = seq // tm
    n_steps = t // tm
    cast_in, cast_out, cast_shape = _cast_plan(cast_sources, n_steps)
    out = pl.pallas_call(
        functools.partial(_outproj_ffn_kernel, n_cast=len(cast_in)),
        grid=(n_steps,),
        in_specs=[
            pl.BlockSpec((tm, d), lambda i: (i, 0)),
            pl.BlockSpec((tm, wa), lambda i: (i, 0)),
            pl.BlockSpec((tm, wb), lambda i: (i, 0)),
            pl.BlockSpec((1, N_MOD, d), lambda i: (i // tiles_per_batch, 0, 0)),
            _vec_spec(wa + wb), _vec_spec(d),
            pl.BlockSpec((wa + wb, d), lambda i: (0, 0), pipeline_mode=pl.Buffered(1)),
            _vec_spec(d), _vec_spec(d),
            *_ffn_specs(d, d_ff),
            *cast_in,
        ],
        out_specs=[pl.BlockSpec((tm, d), lambda i: (i, 0)), *cast_out],
        out_shape=[jax.ShapeDtypeStruct((t, d), F32), *cast_shape],
        compiler_params=_cparams(1),
        name="outproj_swiglu",
    )(x, ya, yb, mod_l, gg, gpost1, w_out, gpre2, gpost2, *w_bf16,
      *[w for w, _ in cast_sources])
    return out[0], tuple(out[1:])


def _t5_bucket_np(dist):
    n = np.maximum(dist, 0)
    max_exact = T5_BUCKETS // 2
    nf = np.maximum(n, 1).astype(np.float32)
    large = max_exact + (np.log(nf / np.float32(max_exact)) / np.float32(math.log(T5_MAX_DIST / max_exact))
                         * np.float32(T5_BUCKETS - max_exact)).astype(np.int32)
    large = np.minimum(large, T5_BUCKETS - 1)
    return np.where(n < max_exact, n, large).astype(np.int32)


def _bias_kernel(tab_ref, *refs, head_base, heads_per_step):
    n = len(refs) // 2
    step = pl.program_id(0)
    for idx_ref, o_ref in zip(refs[:n], refs[n:]):
        idx = idx_ref[...]
        width = idx.shape[1]
        for hh in range(heads_per_step):
            head = head_base + step * heads_per_step + hh
            out = jnp.full(idx.shape, NEG, F32)
            for bucket in range(T5_BUCKETS):
                out = jnp.where(idx == bucket, tab_ref[bucket, head], out)
            o_ref[0, :, hh * width:(hh + 1) * width] = out


def _bias_expand(rel_bias, buckets, head_base, n_steps, heads_per_step):
    shapes = [b.shape for b in buckets]
    return pl.pallas_call(
        functools.partial(_bias_kernel, head_base=head_base, heads_per_step=heads_per_step),
        grid=(n_steps,),
        in_specs=[pl.BlockSpec(memory_space=pltpu.SMEM)]
        + [pl.BlockSpec(s, lambda h: (0, 0)) for s in shapes],
        out_specs=[pl.BlockSpec((1, s[0], heads_per_step * s[1]), lambda h: (h, 0, 0))
                   for s in shapes],
        out_shape=[jax.ShapeDtypeStruct((n_steps, s[0], heads_per_step * s[1]), F32)
                   for s in shapes],
        compiler_params=_cparams(1),
        name="t5_bias_tables",
    )(rel_bias, *[jnp.asarray(b) for b in buckets])


def _bias_tables(rel_bias, n_swa_heads, n_swa_kv_heads):
    n_heads = rel_bias.shape[1]
    kj = np.arange(2 * SWA_BLOCK)[:, None]
    qi = np.arange(SWA_BLOCK)[None, :]
    dist = qi + SWA_BLOCK - kj
    swa_idx = np.where((dist >= 0) & (dist < SWA_WINDOW), _t5_bucket_np(dist), -1).astype(np.int32)
    (swa,) = _bias_expand(rel_bias, [swa_idx], 0, n_swa_kv_heads,
                          n_swa_heads // n_swa_kv_heads)

    kk = np.arange(MOBA_BLOCK)[:, None]
    qq = np.arange(MOBA_BLOCK)[None, :]
    d_own = qq - kk
    own_idx = np.where(d_own >= 0, _t5_bucket_np(d_own), -1).astype(np.int32)
    far = int(_t5_bucket_np(np.array(1 << 30)))
    prev_idx = _t5_bucket_np(d_own + MOBA_BLOCK)
    outside = np.ones_like(prev_idx, dtype=bool)
    outside[MOBA_BLOCK - CORNER:, :CORNER] = False
    assert np.all(prev_idx[outside] == far)
    corner_idx = prev_idx[MOBA_BLOCK - CORNER:, :CORNER]
    n_pairs = (n_heads - n_swa_heads) // 2
    own_t, corner_t = _bias_expand(rel_bias, [own_idx, corner_idx], n_swa_heads, n_pairs, 2)
    const = rel_bias[far, n_swa_heads:]
    corner_t = corner_t - jnp.repeat(const, CORNER).reshape(n_pairs, 1, 2 * CORNER)
    const = jnp.repeat(const, MOBA_BLOCK).reshape(n_pairs, 1, 2 * MOBA_BLOCK)
    return swa, own_t, corner_t, const


def kernel(x, c, rel_bias, ada_w, ada_b, norm_pre, norm_post, ffn_w_gate, ffn_w_up, ffn_w_down,
           mix_w_in, mix_b_in, mix_w_out, attn_sinks, group_gain):
    n_batch, seq, d = x.shape
    depth = ada_w.shape[0]
    n_heads = d // HEAD_DIM
    swa_q_heads = n_heads // 2
    swa_kv_heads = max(1, swa_q_heads // 4)
    moba_heads = n_heads - swa_q_heads
    swa_w = swa_q_heads * HEAD_DIM
    kv_w = swa_kv_heads * HEAD_DIM
    moba_w = moba_heads * HEAD_DIM
    q_col = (swa_w + 2 * kv_w) // LANES
    k_col = q_col + moba_w // LANES
    v_col = k_col + moba_w // LANES

    mod = _modulation(c, ada_w, ada_b).reshape(depth, n_batch, N_MOD, d)
    bias_swa, bias_own_t, bias_corner_t, bias_const = _bias_tables(rel_bias, swa_q_heads, swa_kv_heads)
    sinks = jnp.repeat(attn_sinks, SWA_BLOCK, axis=1).reshape(depth, swa_kv_heads, 1, -1)

    ffn_w = (ffn_w_gate, ffn_w_up, ffn_w_down)
    w_first = tuple(w[0, 0].astype(BF16) for w in ffn_w)
    w_in = mix_w_in[0].astype(BF16)
    b_in = mix_b_in[:, None, :]
    gpre = norm_pre[:, :, None, None, :]
    gpost = norm_post[:, :, None, None, :]
    gg = group_gain[:, None, None, :]

    xt = x.reshape(n_batch * seq, d)
    for l in range(depth):
        xt, (*w_second, w_out) = _ffn(xt, mod[l], gpre[l, 0], gpost[l, 0], w_first,
                                      [(w, (l, 1)) for w in ffn_w] + [(mix_w_out, (l,))], seq)
        qkv = _inproj(xt, mod[l], gpre[l, 1], w_in, b_in, l, seq)
        ya = _swa(qkv, bias_swa, sinks[l], n_batch, seq, swa_q_heads, swa_kv_heads)
        yb = _moba(qkv, bias_own_t, bias_corner_t, bias_const, n_batch, seq, q_col, k_col, v_col,
                   moba_heads)
        nxt = ([(w, (l + 1, 0)) for w in ffn_w] + [(mix_w_in, (l + 1,))]) if l + 1 < depth else []
        xt, nxt = _outproj_ffn(xt, ya, yb, mod[l], gg[l], gpost[l, 1], w_out, gpre[l, 2],
                               gpost[l, 2], w_second, nxt, seq)
        if nxt:
            *w_first, w_in = nxt
    return xt.reshape(n_batch, seq, d)
```

```python
import functools
import math

import numpy as np
import jax
import jax.numpy as jnp
from jax import lax
from jax.experimental import pallas as pl
from jax.experimental.pallas import tpu as pltpu

HEAD_DIM = 64
LANES = 128
SWA_BLOCK = 128
SWA_WINDOW = 128
MOBA_BLOCK = 256
MOBA_TOPK = 3
FAR_GROUPS = (16, 8, 4, 2, 1)
MOBA_Q_BLOCKS = 4
CORNER = 128
BF16_SUBLANES = 16
ONES_ROWS = BF16_SUBLANES
T5_BUCKETS = 32
T5_MAX_DIST = 128
N_MOD = 9
EPS = 1e-6
NEG = -1e30
VMEM_LIMIT = 56 * 1024 * 1024

F32 = jnp.float32
BF16 = jnp.bfloat16


def _cparams(n_axes):
    return pltpu.CompilerParams(
        dimension_semantics=("arbitrary",) * n_axes,
        vmem_limit_bytes=VMEM_LIMIT,
    )


def _rms(x, g):
    return x * lax.rsqrt(jnp.mean(x * x, axis=-1, keepdims=True) + EPS) * g


def _dot(a, b):
    return jnp.dot(a, b, preferred_element_type=F32)


def _dot_nt(a, b):
    return lax.dot_general(a, b, (((1,), (1,)), ((), ())), preferred_element_type=F32)


def _mod_kernel(ct_ref, w_ref, b_ref, o_ref, *, n_batch, row_chunk):
    ct = ct_ref[...]
    ca = ct * jax.nn.sigmoid(ct)
    d = ct.shape[0]
    for b in range(n_batch):
        acc = b_ref[0]
        for r in range(0, d, row_chunk):
            w = w_ref[0, r:r + row_chunk, :]
            acc = acc + jnp.sum(w * ca[r:r + row_chunk, b:b + 1], axis=0, keepdims=True)
        o_ref[0, b:b + 1, :] = acc


def _modulation(c, ada_w, ada_b, *, tn=1152, row_chunk=128):
    depth, d, n = ada_w.shape
    n_batch = c.shape[0]
    return pl.pallas_call(
        functools.partial(_mod_kernel, n_batch=n_batch, row_chunk=row_chunk),
        grid=(depth, n // tn),
        in_specs=[
            pl.BlockSpec((d, n_batch), lambda l, j: (0, 0)),
            pl.BlockSpec((1, d, tn), lambda l, j: (l, 0, j)),
            pl.BlockSpec((1, 1, tn), lambda l, j: (l, 0, j)),
        ],
        out_specs=pl.BlockSpec((1, n_batch, tn), lambda l, j: (l, 0, j)),
        out_shape=jax.ShapeDtypeStruct((depth, n_batch, n), F32),
        compiler_params=_cparams(2),
        name="adaln_modulation",
    )(c.T, ada_w, ada_b.reshape(depth, 1, n))


def _modulate(x, gpre, mod, base):
    return (_rms(x, gpre * (1.0 + mod[base + 1:base + 2])) + mod[base:base + 1]).astype(BF16)


def _swiglu_step(x, mod, base, gpre, gpost, w_refs):
    wg_ref, wu_ref, wd_ref = w_refs
    h = _modulate(x, gpre, mod, base)
    g = _dot(h, wg_ref[...])
    u = _dot(h, wu_ref[...])
    y = _dot((g * jax.nn.sigmoid(g) * u).astype(BF16), wd_ref[...])
    return x + _rms(y, gpost * (0.5 * mod[base + 2:base + 3]))


def _ffn_specs(d, d_ff):
    const = pl.Buffered(1)
    return [
        pl.BlockSpec((d, d_ff), lambda i: (0, 0), pipeline_mode=const),
        pl.BlockSpec((d, d_ff), lambda i: (0, 0), pipeline_mode=const),
        pl.BlockSpec((d_ff, d), lambda i: (0, 0), pipeline_mode=const),
    ]


def _vec_spec(width):
    return pl.BlockSpec((1, 1, width), lambda i: (0, 0, 0))


def _slab_rows(n_rows, n_steps):
    return min(r for r in range(BF16_SUBLANES, n_rows + 1, BF16_SUBLANES)
               if n_rows % r == 0 and r * n_steps >= n_rows)


def _cast_plan(sources, n_steps):
    in_specs, out_specs, out_shape = [], [], []
    for w, lead in sources:
        n_rows, n_cols = w.shape[-2:]
        rows = _slab_rows(n_rows, n_steps)

        def slab(i, last=n_rows // rows - 1):
            return jnp.minimum(i, last)

        in_specs.append(pl.BlockSpec((1,) * len(lead) + (rows, n_cols),
                                     lambda i, slab=slab, lead=lead: (*lead, slab(i), 0)))
        out_specs.append(pl.BlockSpec((rows, n_cols), lambda i, slab=slab: (slab(i), 0)))
        out_shape.append(jax.ShapeDtypeStruct((n_rows, n_cols), BF16))
    return in_specs, out_specs, out_shape


def _cast_slabs(src_refs, dst_refs):
    for src, dst in zip(src_refs, dst_refs):
        dst[...] = src[...].reshape(dst.shape).astype(dst.dtype)


def _ffn_kernel(*refs, n_cast):
    x_ref, mod_ref, gpre_ref, gpost_ref = refs[:4]
    w_refs, src_refs = refs[4:7], refs[7:7 + n_cast]
    o_ref, dst_refs = refs[7 + n_cast], refs[8 + n_cast:]
    o_ref[...] = _swiglu_step(x_ref[...], mod_ref[0], 0, gpre_ref[0], gpost_ref[0], w_refs)
    _cast_slabs(src_refs, dst_refs)


def _ffn(x, mod_l, gpre, gpost, w_bf16, cast_sources, seq, *, tm=512):
    t, d = x.shape
    d_ff = w_bf16[0].shape[-1]
    tiles_per_batch = seq // tm
    n_steps = t // tm
    cast_in, cast_out, cast_shape = _cast_plan(cast_sources, n_steps)
    out = pl.pallas_call(
        functools.partial(_ffn_kernel, n_cast=len(cast_in)),
        grid=(n_steps,),
        in_specs=[
            pl.BlockSpec((tm, d), lambda i: (i, 0)),
            pl.BlockSpec((1, N_MOD, d), lambda i: (i // tiles_per_batch, 0, 0)),
            _vec_spec(d), _vec_spec(d),
            *_ffn_specs(d, d_ff),
            *cast_in,
        ],
        out_specs=[pl.BlockSpec((tm, d), lambda i: (i, 0)), *cast_out],
        out_shape=[jax.ShapeDtypeStruct((t, d), F32), *cast_shape],
        compiler_params=_cparams(1),
        name="swiglu_half_step",
    )(x, mod_l, gpre, gpost, *w_bf16, *[w for w, _ in cast_sources])
    return out[0], tuple(out[1:])


def _inproj_kernel(x_ref, mod_ref, gpre_ref, w_ref, b_ref, o_ref):
    h = _modulate(x_ref[...], gpre_ref[0], mod_ref[0], 3)
    o_ref[...] = (_dot(h, w_ref[...]) + b_ref[0]).astype(o_ref.dtype)


def _inproj(x, mod_l, gpre, w_in, b_in, layer, seq, *, tm=512):
    t, d = x.shape
    n = w_in.shape[-1]
    tiles_per_batch = seq // tm
    return pl.pallas_call(
        _inproj_kernel,
        grid=(t // tm,),
        in_specs=[
            pl.BlockSpec((tm, d), lambda i: (i, 0)),
            pl.BlockSpec((1, N_MOD, d), lambda i: (i // tiles_per_batch, 0, 0)),
            _vec_spec(d),
            pl.BlockSpec((d, n), lambda i: (0, 0), pipeline_mode=pl.Buffered(1)),
            pl.BlockSpec((1, 1, n), lambda i: (layer, 0, 0)),
        ],
        out_specs=pl.BlockSpec((tm, n), lambda i: (i, 0)),
        out_shape=jax.ShapeDtypeStruct((t, n), BF16),
        compiler_params=_cparams(1),
        name="qkv_projection",
    )(x, mod_l, gpre, w_in, b_in)


def _swa_kernel(q_ref, kp_ref, km_ref, vp_ref, vm_ref, bias_ref, sink_ref, o_ref, *, group):
    tq = q_ref.shape[0]
    W = 2 * SWA_BLOCK
    i = pl.program_id(1)
    k_all = jnp.concatenate([kp_ref[...], km_ref[...]], axis=0)
    v_all = jnp.concatenate([vp_ref[...], vm_ref[...]], axis=0)
    vt_all = v_all.astype(F32).T.astype(BF16)
    ones = jnp.ones((ONES_ROWS, v_all.shape[0]), BF16)
    vt = [jnp.concatenate([vt_all[kv * HEAD_DIM:(kv + 1) * HEAD_DIM], ones], axis=0)
          for kv in range(2)]
    feat_lo = lax.broadcasted_iota(jnp.int32, (LANES, SWA_BLOCK), 0) < HEAD_DIM
    key_row = lax.broadcasted_iota(jnp.int32, (W, group * SWA_BLOCK), 0)
    no_prev = jnp.logical_and(key_row < SWA_BLOCK, i == 0)
    scale = HEAD_DIM ** -0.5

    def swap_halves(x):
        return jnp.concatenate([x[HEAD_DIM:], x[:HEAD_DIM]], axis=0)

    for blk in range(tq // SWA_BLOCK):
        r0 = blk * SWA_BLOCK
        kw = k_all[r0:r0 + W]
        for kv in range(2):
            cols = []
            for g in range(group):
                h = kv * group + g
                pair, odd = h // 2, h % 2
                qpt = (q_ref[r0:r0 + SWA_BLOCK, pair * LANES:(pair + 1) * LANES].astype(F32)
                       * scale).T
                if odd != kv:
                    qpt = swap_halves(qpt)
                keep = feat_lo if kv == 0 else jnp.logical_not(feat_lo)
                cols.append(jnp.where(keep, qpt, jnp.zeros_like(qpt)))
            qt = jnp.concatenate(cols, axis=1).astype(BF16)
            st = _dot(kw, qt)
            st = st + bias_ref[kv]
            if blk == 0:
                st = jnp.where(no_prev, NEG, st)
            sink = sink_ref[kv]
            m = jnp.maximum(jnp.max(st, axis=0, keepdims=True), sink)
            acc = _dot(vt[kv][:, r0:r0 + W], jnp.exp(st - m).astype(BF16))
            out_t = acc[:HEAD_DIM] * (1.0 / (acc[HEAD_DIM:HEAD_DIM + 1] + jnp.exp(sink - m)))
            for gp in range(group // 2):
                c0 = 2 * gp * SWA_BLOCK
                pair_t = jnp.concatenate(
                    [out_t[:, c0:c0 + SWA_BLOCK], out_t[:, c0 + SWA_BLOCK:c0 + 2 * SWA_BLOCK]],
                    axis=0)
                pair = (kv * group) // 2 + gp
                o_ref[r0:r0 + SWA_BLOCK, pair * LANES:(pair + 1) * LANES] = pair_t.T.astype(
                    o_ref.dtype)


def _swa(qkv, bias, sinks, n_batch, seq, n_q_heads, n_kv_heads, *, tq=1024):
    t = qkv.shape[0]
    nq = seq // tq
    qw = n_q_heads * HEAD_DIM
    k_col = qw // LANES
    v_col = k_col + (n_kv_heads * HEAD_DIM) // LANES
    per = tq // SWA_BLOCK
    group = n_q_heads // n_kv_heads
    assert n_kv_heads * HEAD_DIM == LANES and group % 2 == 0

    def prev_map(col):
        return lambda b, i: (jnp.maximum((b * nq + i) * per - 1, 0), col)

    return pl.pallas_call(
        functools.partial(_swa_kernel, group=group),
        grid=(n_batch, nq),
        in_specs=[
            pl.BlockSpec((tq, qw), lambda b, i: (b * nq + i, 0)),
            pl.BlockSpec((SWA_BLOCK, LANES), prev_map(k_col)),
            pl.BlockSpec((tq, LANES), lambda b, i: (b * nq + i, k_col)),
            pl.BlockSpec((SWA_BLOCK, LANES), prev_map(v_col)),
            pl.BlockSpec((tq, LANES), lambda b, i: (b * nq + i, v_col)),
            pl.BlockSpec((n_kv_heads, 2 * SWA_BLOCK, group * SWA_BLOCK), lambda b, i: (0, 0, 0)),
            pl.BlockSpec((n_kv_heads, 1, group * SWA_BLOCK), lambda b, i: (0, 0, 0)),
        ],
        out_specs=pl.BlockSpec((tq, qw), lambda b, i: (b * nq + i, 0)),
        out_shape=jax.ShapeDtypeStruct((t, qw), BF16),
        compiler_params=_cparams(2),
        name="swa_attention",
    )(qkv, qkv, qkv, qkv, qkv, bias, sinks)


def _moba_kernel(q_ref, k_ref, v_ref, bown_ref, bcorner_ref, bconst_ref, o_ref, kmean_ref, vt_ref,
                 st_ref, *, n_blocks):
    L = MOBA_BLOCK
    step = pl.program_id(2)

    @pl.when(step == 0)
    def _():
        kmean_ref[...] = jnp.zeros_like(kmean_ref)
        for jb in range(n_blocks):
            kb = k_ref[jb * L:(jb + 1) * L, :].astype(F32)
            kmean_ref[jb:jb + 1, :] = jnp.sum(kb, axis=0, keepdims=True) * (1.0 / L)
            vt = v_ref[jb * L:(jb + 1) * L, :].astype(F32).T.astype(BF16)
            for head in range(2):
                vt_ref[jb, head, :HEAD_DIM, :] = vt[head * HEAD_DIM:(head + 1) * HEAD_DIM]
                vt_ref[jb, head, HEAD_DIM:, :] = jnp.ones((ONES_ROWS, L), BF16)

    blocks = [_moba_query_block(step * MOBA_Q_BLOCKS + sub, q_ref[sub * L:(sub + 1) * L, :],
                                k_ref, bown_ref, bcorner_ref, bconst_ref, kmean_ref, vt_ref,
                                st_ref, n_blocks) for sub in range(MOBA_Q_BLOCKS)]
    for blk in blocks:
        next(blk)
    for sub, blk in enumerate(blocks):
        next(blk)
        next(blk)
        o_ref[sub * L:(sub + 1) * L, :] = next(blk).astype(o_ref.dtype)


def _moba_query_block(c, q, k_ref, bown_ref, bcorner_ref, bconst_ref, kmean_ref, vt_ref, st_ref,
                      n_blocks):
    L = MOBA_BLOCK
    scale = HEAD_DIM ** -0.5
    qt = (q.astype(F32) * scale).T
    lo = lax.broadcasted_iota(jnp.int32, (LANES, L), 0) < HEAD_DIM
    zero = jnp.zeros_like(qt)
    q2t = jnp.concatenate([jnp.where(lo, qt, zero), jnp.where(lo, zero, qt)],
                          axis=1).astype(BF16)
    cols = 2 * L

    km = kmean_ref[...]
    km_hi = km.astype(BF16)
    km_lo = (km - km_hi.astype(F32)).astype(BF16)
    gate = _dot(km_hi, q2t) + _dot(km_lo, q2t)
    rowf = lax.broadcasted_iota(jnp.int32, gate.shape, 0).astype(F32)
    g = jnp.where(rowf < c.astype(F32), gate, NEG)
    sels = []
    for t in range(MOBA_TOPK):
        gmax = jnp.max(g, axis=0, keepdims=True)
        idx = jnp.min(jnp.where(g == gmax, rowf, float(n_blocks)), axis=0, keepdims=True)
        sels.append(jnp.where(t < c, idx, -1.0))
        g = jnp.where(rowf == idx, -jnp.inf, g)

    def selected(jf):
        hit = sels[0] == jf
        for s_t in sels[1:]:
            hit = jnp.logical_or(hit, s_t == jf)
        return hit

    def scores(j, n=1):
        return _dot(k_ref[pl.ds(pl.multiple_of(j * L, L), n * L), :], q2t)

    def colmax(st):
        return jnp.max(st, axis=0, keepdims=True)

    st_own = scores(c) + bown_ref[0]
    m_own = colmax(st_own)
    yield

    def far_scores(j0, n, m_far):
        st = scores(j0, n)
        for g in range(n):
            stg = jnp.where(selected((j0 + g).astype(F32)), st[g * L:(g + 1) * L], NEG)
            st_ref[j0 + g] = stg
            m_far = jnp.maximum(m_far, colmax(stg))
        return m_far

    n_far = c

    def far_sweep(fn, carry):
        start = jnp.int32(0)
        for n in FAR_GROUPS:
            count = lax.shift_right_logical(n_far - start, int(math.log2(n)))
            carry = lax.fori_loop(0, count, lambda i, cr, s=start, n=n: fn(s + i * n, n, cr),
                                  carry)
            start = start + count * n
        return carry

    m_far = far_sweep(far_scores, jnp.full((1, cols), NEG, F32))

    def fix_prev(_):
        tops = []
        for head in range(2):
            cs = slice(head * L, head * L + CORNER)
            blk = (st_ref[c - 1, L - CORNER:, cs]
                   + bcorner_ref[0, :, head * CORNER:(head + 1) * CORNER])
            st_ref[c - 1, L - CORNER:, cs] = blk
            tops += [colmax(blk), jnp.full((1, L - CORNER), NEG, F32)]
        return jnp.concatenate(tops, axis=1)

    m_corner = lax.cond(c > 0, fix_prev, lambda _: jnp.full((1, cols), NEG, F32), 0)
    yield
    bconst = bconst_ref[0]
    m = jnp.maximum(m_own, jnp.maximum(m_far, m_corner) + bconst)

    def add_pv(acc, j, x):
        p = jnp.exp(x.astype(BF16))
        pv = [_dot(vt_ref[j, head], p[:, head * L:(head + 1) * L]) for head in range(2)]
        return pv if acc is None else [a + b for a, b in zip(acc, pv)]

    acc = add_pv(None, c, st_own - m)
    m_off = m - bconst
    yield

    def far_pv(j0, n, acc):
        for g in range(n):
            acc = add_pv(acc, j0 + g, st_ref[j0 + g] - m_off)
        return acc

    acc = far_sweep(far_pv, acc)
    out_t = jnp.concatenate([a[:HEAD_DIM] * (1.0 / a[HEAD_DIM:HEAD_DIM + 1]) for a in acc],
                            axis=0)
    yield out_t.T


def _moba(qkv, bown_t, bcorner_t, bconst, n_batch, seq, q_col, k_col, v_col, n_heads):
    t = qkv.shape[0]
    L = MOBA_BLOCK
    nb = seq // L
    nbp = -(-nb // 8) * 8
    n_pairs = n_heads // 2
    assert nb % MOBA_Q_BLOCKS == 0
    steps = nb // MOBA_Q_BLOCKS
    return pl.pallas_call(
        functools.partial(_moba_kernel, n_blocks=nb),
        grid=(n_batch, n_pairs, steps),
        in_specs=[
            pl.BlockSpec((MOBA_Q_BLOCKS * L, LANES), lambda b, p, c: (b * steps + c, q_col + p)),
            pl.BlockSpec((seq, LANES), lambda b, p, c: (b, k_col + p)),
            pl.BlockSpec((seq, LANES), lambda b, p, c: (b, v_col + p)),
            pl.BlockSpec((1, L, 2 * L), lambda b, p, c: (p, 0, 0)),
            pl.BlockSpec((1, CORNER, 2 * CORNER), lambda b, p, c: (p, 0, 0)),
            pl.BlockSpec((1, 1, 2 * L), lambda b, p, c: (p, 0, 0)),
        ],
        out_specs=pl.BlockSpec((MOBA_Q_BLOCKS * L, LANES), lambda b, p, c: (b * steps + c, p)),
        out_shape=jax.ShapeDtypeStruct((t, n_heads * HEAD_DIM), BF16),
        scratch_shapes=[pltpu.VMEM((nbp, LANES), F32),
                        pltpu.VMEM((nb, 2, HEAD_DIM + ONES_ROWS, L), BF16),
                        pltpu.VMEM((nb, L, 2 * L), F32)],
        compiler_params=_cparams(3),
        name="moba_attention",
    )(qkv, qkv, qkv, bown_t, bcorner_t, bconst)


def _outproj_ffn_kernel(*refs, n_cast):
    x_ref, ya_ref, yb_ref, mod_ref, gg_ref, gpost1_ref, w_ref, gpre2_ref, gpost2_ref = refs[:9]
    w_refs, src_refs = refs[9:12], refs[12:12 + n_cast]
    o_ref, dst_refs = refs[12 + n_cast], refs[13 + n_cast:]
    wa = ya_ref.shape[1]
    mod = mod_ref[0]
    gg = gg_ref[0]
    ya = _rms(ya_ref[...].astype(F32), gg[:, :wa]).astype(BF16)
    yb = _rms(yb_ref[...].astype(F32), gg[:, wa:]).astype(BF16)
    y = _dot(ya, w_ref[:wa, :]) + _dot(yb, w_ref[wa:, :])
    x = x_ref[...] + _rms(y, gpost1_ref[0] * mod[5:6])
    o_ref[...] = _swiglu_step(x, mod, 6, gpre2_ref[0], gpost2_ref[0], w_refs)
    _cast_slabs(src_refs, dst_refs)


def _outproj_ffn(x, ya, yb, mod_l, gg, gpost1, w_out, gpre2, gpost2, w_bf16, cast_sources, seq,
                 *, tm=512):
    t, d = x.shape
    d_ff = w_bf16[0].shape[-1]
    wa, wb = ya.shape[1], yb.shape[1]
    tiles_per_batch = seq // tm
    n_steps = t // tm
    cast_in, cast_out, cast_shape = _cast_plan(cast_sources, n_steps)
    out = pl.pallas_call(
        functools.partial(_outproj_ffn_kernel, n_cast=len(cast_in)),
        grid=(n_steps,),
        in_specs=[
            pl.BlockSpec((tm, d), lambda i: (i, 0)),
            pl.BlockSpec((tm, wa), lambda i: (i, 0)),
            pl.BlockSpec((tm, wb), lambda i: (i, 0)),
            pl.BlockSpec((1, N_MOD, d), lambda i: (i // tiles_per_batch, 0, 0)),
            _vec_spec(wa + wb), _vec_spec(d),
            pl.BlockSpec((wa + wb, d), lambda i: (0, 0), pipeline_mode=pl.Buffered(1)),
            _vec_spec(d), _vec_spec(d),
            *_ffn_specs(d, d_ff),
            *cast_in,
        ],
        out_specs=[pl.BlockSpec((tm, d), lambda i: (i, 0)), *cast_out],
        out_shape=[jax.ShapeDtypeStruct((t, d), F32), *cast_shape],
        compiler_params=_cparams(1),
        name="outproj_swiglu",
    )(x, ya, yb, mod_l, gg, gpost1, w_out, gpre2, gpost2, *w_bf16,
      *[w for w, _ in cast_sources])
    return out[0], tuple(out[1:])


def _t5_bucket_np(dist):
    n = np.maximum(dist, 0)
    max_exact = T5_BUCKETS // 2
    nf = np.maximum(n, 1).astype(np.float32)
    large = max_exact + (np.log(nf / np.float32(max_exact)) / np.float32(math.log(T5_MAX_DIST / max_exact))
                         * np.float32(T5_BUCKETS - max_exact)).astype(np.int32)
    large = np.minimum(large, T5_BUCKETS - 1)
    return np.where(n < max_exact, n, large).astype(np.int32)


def _bias_kernel(tab_ref, *refs, head_base, heads_per_step):
    n = len(refs) // 2
    step = pl.program_id(0)
    for idx_ref, o_ref in zip(refs[:n], refs[n:]):
        idx = idx_ref[...]
        width = idx.shape[1]
        for hh in range(heads_per_step):
            head = head_base + step * heads_per_step + hh
            out = jnp.full(idx.shape, NEG, F32)
            for bucket in range(T5_BUCKETS):
                out = jnp.where(idx == bucket, tab_ref[bucket, head], out)
            o_ref[0, :, hh * width:(hh + 1) * width] = out


def _bias_expand(rel_bias, buckets, head_base, n_steps, heads_per_step):
    shapes = [b.shape for b in buckets]
    return pl.pallas_call(
        functools.partial(_bias_kernel, head_base=head_base, heads_per_step=heads_per_step),
        grid=(n_steps,),
        in_specs=[pl.BlockSpec(memory_space=pltpu.SMEM)]
        + [pl.BlockSpec(s, lambda h: (0, 0)) for s in shapes],
        out_specs=[pl.BlockSpec((1, s[0], heads_per_step * s[1]), lambda h: (h, 0, 0))
                   for s in shapes],
        out_shape=[jax.ShapeDtypeStruct((n_steps, s[0], heads_per_step * s[1]), F32)
                   for s in shapes],
        compiler_params=_cparams(1),
        name="t5_bias_tables",
    )(rel_bias, *[jnp.asarray(b) for b in buckets])


def _bias_tables(rel_bias, n_swa_heads, n_swa_kv_heads):
    n_heads = rel_bias.shape[1]
    kj = np.arange(2 * SWA_BLOCK)[:, None]
    qi = np.arange(SWA_BLOCK)[None, :]
    dist = qi + SWA_BLOCK - kj
    swa_idx = np.where((dist >= 0) & (dist < SWA_WINDOW), _t5_bucket_np(dist), -1).astype(np.int32)
    (swa,) = _bias_expand(rel_bias, [swa_idx], 0, n_swa_kv_heads,
                          n_swa_heads // n_swa_kv_heads)

    kk = np.arange(MOBA_BLOCK)[:, None]
    qq = np.arange(MOBA_BLOCK)[None, :]
    d_own = qq - kk
    own_idx = np.where(d_own >= 0, _t5_bucket_np(d_own), -1).astype(np.int32)
    far = int(_t5_bucket_np(np.array(1 << 30)))
    prev_idx = _t5_bucket_np(d_own + MOBA_BLOCK)
    outside = np.ones_like(prev_idx, dtype=bool)
    outside[MOBA_BLOCK - CORNER:, :CORNER] = False
    assert np.all(prev_idx[outside] == far)
    corner_idx = prev_idx[MOBA_BLOCK - CORNER:, :CORNER]
    n_pairs = (n_heads - n_swa_heads) // 2
    own_t, corner_t = _bias_expand(rel_bias, [own_idx, corner_idx], n_swa_heads, n_pairs, 2)
    const = rel_bias[far, n_swa_heads:]
    corner_t = corner_t - jnp.repeat(const, CORNER).reshape(n_pairs, 1, 2 * CORNER)
    const = jnp.repeat(const, MOBA_BLOCK).reshape(n_pairs, 1, 2 * MOBA_BLOCK)
    return swa, own_t, corner_t, const


def kernel(x, c, rel_bias, ada_w, ada_b, norm_pre, norm_post, ffn_w_gate, ffn_w_up, ffn_w_down,
           mix_w_in, mix_b_in, mix_w_out, attn_sinks, group_gain):
    n_batch, seq, d = x.shape
    depth = ada_w.shape[0]
    n_heads = d // HEAD_DIM
    swa_q_heads = n_heads // 2
    swa_kv_heads = max(1, swa_q_heads // 4)
    moba_heads = n_heads - swa_q_heads
    swa_w = swa_q_heads * HEAD_DIM
    kv_w = swa_kv_heads * HEAD_DIM
    moba_w = moba_heads * HEAD_DIM
    q_col = (swa_w + 2 * kv_w) // LANES
    k_col = q_col + moba_w // LANES
    v_col = k_col + moba_w // LANES

    mod = _modulation(c, ada_w, ada_b).reshape(depth, n_batch, N_MOD, d)
    bias_swa, bias_own_t, bias_corner_t, bias_const = _bias_tables(rel_bias, swa_q_heads, swa_kv_heads)
    sinks = jnp.repeat(attn_sinks, SWA_BLOCK, axis=1).reshape(depth, swa_kv_heads, 1, -1)

    ffn_w = (ffn_w_gate, ffn_w_up, ffn_w_down)
    w_first = tuple(w[0, 0].astype(BF16) for w in ffn_w)
    w_in = mix_w_in[0].astype(BF16)
    b_in = mix_b_in[:, None, :]
    gpre = norm_pre[:, :, None, None, :]
    gpost = norm_post[:, :, None, None, :]
    gg = group_gain[:, None, None, :]

    xt = x.reshape(n_batch * seq, d)
    for l in range(depth):
        xt, (*w_second, w_out) = _ffn(xt, mod[l], gpre[l, 0], gpost[l, 0], w_first,
                                      [(w, (l, 1)) for w in ffn_w] + [(mix_w_out, (l,))], seq)
        qkv = _inproj(xt, mod[l], gpre[l, 1], w_in, b_in, l, seq)
        ya = _swa(qkv, bias_swa, sinks[l], n_batch, seq, swa_q_heads, swa_kv_heads)
        yb = _moba(qkv, bias_own_t, bias_corner_t, bias_const, n_batch, seq, q_col, k_col, v_col,
                   moba_heads)
        nxt = ([(w, (l + 1, 0)) for w in ffn_w] + [(mix_w_in, (l + 1,))]) if l + 1 < depth else []
        xt, nxt = _outproj_ffn(xt, ya, yb, mod[l], gg[l], gpost[l, 1], w_out, gpre[l, 2],
                               gpost[l, 2], w_second, nxt, seq)
        if nxt:
            *w_first, w_in = nxt
    return xt.reshape(n_batch, seq, d)
```

```python
import functools
import math

import numpy as np
import jax
import jax.numpy as jnp
from jax import lax
from jax.experimental import pallas as pl
from jax.experimental.pallas import tpu as pltpu

HEAD_DIM = 64
LANES = 128
SWA_BLOCK = 128
SWA_WINDOW = 128
MOBA_BLOCK = 256
MOBA_TOPK = 3
FAR_GROUPS = (16, 8, 4, 2, 1)
MOBA_Q_BLOCKS = 8
CORNER = 128
BF16_SUBLANES = 16
ONES_ROWS = BF16_SUBLANES
T5_BUCKETS = 32
T5_MAX_DIST = 128
N_MOD = 9
EPS = 1e-6
NEG = -1e30
VMEM_LIMIT = 56 * 1024 * 1024

F32 = jnp.float32
BF16 = jnp.bfloat16


def _cparams(n_axes):
    return pltpu.CompilerParams(
        dimension_semantics=("arbitrary",) * n_axes,
        vmem_limit_bytes=VMEM_LIMIT,
    )


def _rms(x, g):
    return x * lax.rsqrt(jnp.mean(x * x, axis=-1, keepdims=True) + EPS) * g


def _dot(a, b):
    return jnp.dot(a, b, preferred_element_type=F32)


def _dot_nt(a, b):
    return lax.dot_general(a, b, (((1,), (1,)), ((), ())), preferred_element_type=F32)


def _mod_kernel(ct_ref, w_ref, b_ref, o_ref, *, n_batch, row_chunk):
    ct = ct_ref[...]
    ca = ct * jax.nn.sigmoid(ct)
    d = ct.shape[0]
    for b in range(n_batch):
        acc = b_ref[0]
        for r in range(0, d, row_chunk):
            w = w_ref[0, r:r + row_chunk, :]
            acc = acc + jnp.sum(w * ca[r:r + row_chunk, b:b + 1], axis=0, keepdims=True)
        o_ref[0, b:b + 1, :] = acc


def _modulation(c, ada_w, ada_b, *, tn=1152, row_chunk=128):
    depth, d, n = ada_w.shape
    n_batch = c.shape[0]
    return pl.pallas_call(
        functools.partial(_mod_kernel, n_batch=n_batch, row_chunk=row_chunk),
        grid=(depth, n // tn),
        in_specs=[
            pl.BlockSpec((d, n_batch), lambda l, j: (0, 0)),
            pl.BlockSpec((1, d, tn), lambda l, j: (l, 0, j)),
            pl.BlockSpec((1, 1, tn), lambda l, j: (l, 0, j)),
        ],
        out_specs=pl.BlockSpec((1, n_batch, tn), lambda l, j: (l, 0, j)),
        out_shape=jax.ShapeDtypeStruct((depth, n_batch, n), F32),
        compiler_params=_cparams(2),
        name="adaln_modulation",
    )(c.T, ada_w, ada_b.reshape(depth, 1, n))


def _modulate(x, gpre, mod, base):
    return (_rms(x, gpre * (1.0 + mod[base + 1:base + 2])) + mod[base:base + 1]).astype(BF16)


def _swiglu_step(x, mod, base, gpre, gpost, w_refs):
    wg_ref, wu_ref, wd_ref = w_refs
    h = _modulate(x, gpre, mod, base)
    g = _dot(h, wg_ref[...])
    u = _dot(h, wu_ref[...])
    y = _dot((g * jax.nn.sigmoid(g) * u).astype(BF16), wd_ref[...])
    return x + _rms(y, gpost * (0.5 * mod[base + 2:base + 3]))


def _ffn_specs(d, d_ff):
    const = pl.Buffered(1)
    return [
        pl.BlockSpec((d, d_ff), lambda i: (0, 0), pipeline_mode=const),
        pl.BlockSpec((d, d_ff), lambda i: (0, 0), pipeline_mode=const),
        pl.BlockSpec((d_ff, d), lambda i: (0, 0), pipeline_mode=const),
    ]


def _vec_spec(width):
    return pl.BlockSpec((1, 1, width), lambda i: (0, 0, 0))


def _slab_rows(n_rows, n_steps):
    return min(r for r in range(BF16_SUBLANES, n_rows + 1, BF16_SUBLANES)
               if n_rows % r == 0 and r * n_steps >= n_rows)


def _cast_plan(sources, n_steps):
    in_specs, out_specs, out_shape = [], [], []
    for w, lead in sources:
        n_rows, n_cols = w.shape[-2:]
        rows = _slab_rows(n_rows, n_steps)

        def slab(i, last=n_rows // rows - 1):
            return jnp.minimum(i, last)

        in_specs.append(pl.BlockSpec((1,) * len(lead) + (rows, n_cols),
                                     lambda i, slab=slab, lead=lead: (*lead, slab(i), 0)))
        out_specs.append(pl.BlockSpec((rows, n_cols), lambda i, slab=slab: (slab(i), 0)))
        out_shape.append(jax.ShapeDtypeStruct((n_rows, n_cols), BF16))
    return in_specs, out_specs, out_shape


def _cast_slabs(src_refs, dst_refs):
    for src, dst in zip(src_refs, dst_refs):
        dst[...] = src[...].reshape(dst.shape).astype(dst.dtype)


def _ffn_kernel(*refs, n_cast):
    x_ref, mod_ref, gpre_ref, gpost_ref = refs[:4]
    w_refs, src_refs = refs[4:7], refs[7:7 + n_cast]
    o_ref, dst_refs = refs[7 + n_cast], refs[8 + n_cast:]
    o_ref[...] = _swiglu_step(x_ref[...], mod_ref[0], 0, gpre_ref[0], gpost_ref[0], w_refs)
    _cast_slabs(src_refs, dst_refs)


def _ffn(x, mod_l, gpre, gpost, w_bf16, cast_sources, seq, *, tm=512):
    t, d = x.shape
    d_ff = w_bf16[0].shape[-1]
    tiles_per_batch = seq // tm
    n_steps = t // tm
    cast_in, cast_out, cast_shape = _cast_plan(cast_sources, n_steps)
    out = pl.pallas_call(
        functools.partial(_ffn_kernel, n_cast=len(cast_in)),
        grid=(n_steps,),
        in_specs=[
            pl.BlockSpec((tm, d), lambda i: (i, 0)),
            pl.BlockSpec((1, N_MOD, d), lambda i: (i // tiles_per_batch, 0, 0)),
            _vec_spec(d), _vec_spec(d),
            *_ffn_specs(d, d_ff),
            *cast_in,
        ],
        out_specs=[pl.BlockSpec((tm, d), lambda i: (i, 0)), *cast_out],
        out_shape=[jax.ShapeDtypeStruct((t, d), F32), *cast_shape],
        compiler_params=_cparams(1),
        name="swiglu_half_step",
    )(x, mod_l, gpre, gpost, *w_bf16, *[w for w, _ in cast_sources])
    return out[0], tuple(out[1:])


def _inproj_kernel(x_ref, mod_ref, gpre_ref, w_ref, b_ref, o_ref):
    h = _modulate(x_ref[...], gpre_ref[0], mod_ref[0], 3)
    o_ref[...] = (_dot(h, w_ref[...]) + b_ref[0]).astype(o_ref.dtype)


def _inproj(x, mod_l, gpre, w_in, b_in, layer, seq, *, tm=512):
    t, d = x.shape
    n = w_in.shape[-1]
    tiles_per_batch = seq // tm
    return pl.pallas_call(
        _inproj_kernel,
        grid=(t // tm,),
        in_specs=[
            pl.BlockSpec((tm, d), lambda i: (i, 0)),
            pl.BlockSpec((1, N_MOD, d), lambda i: (i // tiles_per_batch, 0, 0)),
            _vec_spec(d),
            pl.BlockSpec((d, n), lambda i: (0, 0), pipeline_mode=pl.Buffered(1)),
            pl.BlockSpec((1, 1, n), lambda i: (layer, 0, 0)),
        ],
        out_specs=pl.BlockSpec((tm, n), lambda i: (i, 0)),
        out_shape=jax.ShapeDtypeStruct((t, n), BF16),
        compiler_params=_cparams(1),
        name="qkv_projection",
    )(x, mod_l, gpre, w_in, b_in)


def _swa_kernel(q_ref, kp_ref, km_ref, vp_ref, vm_ref, bias_ref, sink_ref, o_ref, *, group):
    tq = q_ref.shape[0]
    W = 2 * SWA_BLOCK
    i = pl.program_id(1)
    k_all = jnp.concatenate([kp_ref[...], km_ref[...]], axis=0)
    v_all = jnp.concatenate([vp_ref[...], vm_ref[...]], axis=0)
    vt_all = v_all.astype(F32).T.astype(BF16)
    ones = jnp.ones((ONES_ROWS, v_all.shape[0]), BF16)
    vt = [jnp.concatenate([vt_all[kv * HEAD_DIM:(kv + 1) * HEAD_DIM], ones], axis=0)
          for kv in range(2)]
    feat_lo = lax.broadcasted_iota(jnp.int32, (LANES, SWA_BLOCK), 0) < HEAD_DIM
    key_row = lax.broadcasted_iota(jnp.int32, (W, group * SWA_BLOCK), 0)
    no_prev = jnp.logical_and(key_row < SWA_BLOCK, i == 0)
    scale = HEAD_DIM ** -0.5

    def swap_halves(x):
        return jnp.concatenate([x[HEAD_DIM:], x[:HEAD_DIM]], axis=0)

    for blk in range(tq // SWA_BLOCK):
        r0 = blk * SWA_BLOCK
        kw = k_all[r0:r0 + W]
        for kv in range(2):
            cols = []
            for g in range(group):
                h = kv * group + g
                pair, odd = h // 2, h % 2
                qpt = (q_ref[r0:r0 + SWA_BLOCK, pair * LANES:(pair + 1) * LANES].astype(F32)
                       * scale).T
                if odd != kv:
                    qpt = swap_halves(qpt)
                keep = feat_lo if kv == 0 else jnp.logical_not(feat_lo)
                cols.append(jnp.where(keep, qpt, jnp.zeros_like(qpt)))
            qt = jnp.concatenate(cols, axis=1).astype(BF16)
            st = _dot(kw, qt)
            st = st + bias_ref[kv]
            if blk == 0:
                st = jnp.where(no_prev, NEG, st)
            sink = sink_ref[kv]
            m = jnp.maximum(jnp.max(st, axis=0, keepdims=True), sink)
            acc = _dot(vt[kv][:, r0:r0 + W], jnp.exp(st - m).astype(BF16))
            out_t = acc[:HEAD_DIM] * (1.0 / (acc[HEAD_DIM:HEAD_DIM + 1] + jnp.exp(sink - m)))
            for gp in range(group // 2):
                c0 = 2 * gp * SWA_BLOCK
                pair_t = jnp.concatenate(
                    [out_t[:, c0:c0 + SWA_BLOCK], out_t[:, c0 + SWA_BLOCK:c0 + 2 * SWA_BLOCK]],
                    axis=0)
                pair = (kv * group) // 2 + gp
                o_ref[r0:r0 + SWA_BLOCK, pair * LANES:(pair + 1) * LANES] = pair_t.T.astype(
                    o_ref.dtype)


def _swa(qkv, bias, sinks, n_batch, seq, n_q_heads, n_kv_heads, *, tq=1024):
    t = qkv.shape[0]
    nq = seq // tq
    qw = n_q_heads * HEAD_DIM
    k_col = qw // LANES
    v_col = k_col + (n_kv_heads * HEAD_DIM) // LANES
    per = tq // SWA_BLOCK
    group = n_q_heads // n_kv_heads
    assert n_kv_heads * HEAD_DIM == LANES and group % 2 == 0

    def prev_map(col):
        return lambda b, i: (jnp.maximum((b * nq + i) * per - 1, 0), col)

    return pl.pallas_call(
        functools.partial(_swa_kernel, group=group),
        grid=(n_batch, nq),
        in_specs=[
            pl.BlockSpec((tq, qw), lambda b, i: (b * nq + i, 0)),
            pl.BlockSpec((SWA_BLOCK, LANES), prev_map(k_col)),
            pl.BlockSpec((tq, LANES), lambda b, i: (b * nq + i, k_col)),
            pl.BlockSpec((SWA_BLOCK, LANES), prev_map(v_col)),
            pl.BlockSpec((tq, LANES), lambda b, i: (b * nq + i, v_col)),
            pl.BlockSpec((n_kv_heads, 2 * SWA_BLOCK, group * SWA_BLOCK), lambda b, i: (0, 0, 0)),
            pl.BlockSpec((n_kv_heads, 1, group * SWA_BLOCK), lambda b, i: (0, 0, 0)),
        ],
        out_specs=pl.BlockSpec((tq, qw), lambda b, i: (b * nq + i, 0)),
        out_shape=jax.ShapeDtypeStruct((t, qw), BF16),
        compiler_params=_cparams(2),
        name="swa_attention",
    )(qkv, qkv, qkv, qkv, qkv, bias, sinks)


def _moba_kernel(q_ref, k_ref, v_ref, bown_ref, bcorner_ref, bconst_ref, o_ref, kmean_ref, vt_ref,
                 st_ref, *, n_blocks):
    L = MOBA_BLOCK
    step = pl.program_id(2)

    @pl.when(step == 0)
    def _():
        kmean_ref[...] = jnp.zeros_like(kmean_ref)
        for jb in range(n_blocks):
            kb = k_ref[jb * L:(jb + 1) * L, :].astype(F32)
            kmean_ref[jb:jb + 1, :] = jnp.sum(kb, axis=0, keepdims=True) * (1.0 / L)
            vt = v_ref[jb * L:(jb + 1) * L, :].astype(F32).T.astype(BF16)
            for head in range(2):
                vt_ref[jb, head, :HEAD_DIM, :] = vt[head * HEAD_DIM:(head + 1) * HEAD_DIM]
                vt_ref[jb, head, HEAD_DIM:, :] = jnp.ones((ONES_ROWS, L), BF16)

    blocks = [_moba_query_block(step * MOBA_Q_BLOCKS + sub, q_ref[sub * L:(sub + 1) * L, :],
                                k_ref, bown_ref, bcorner_ref, bconst_ref, kmean_ref, vt_ref,
                                st_ref, n_blocks) for sub in range(MOBA_Q_BLOCKS)]
    for blk in blocks:
        next(blk)
    for sub, blk in enumerate(blocks):
        next(blk)
        next(blk)
        o_ref[sub * L:(sub + 1) * L, :] = next(blk).astype(o_ref.dtype)


def _moba_query_block(c, q, k_ref, bown_ref, bcorner_ref, bconst_ref, kmean_ref, vt_ref, st_ref,
                      n_blocks):
    L = MOBA_BLOCK
    scale = HEAD_DIM ** -0.5
    qt = (q.astype(F32) * scale).T
    lo = lax.broadcasted_iota(jnp.int32, (LANES, L), 0) < HEAD_DIM
    zero = jnp.zeros_like(qt)
    q2t = jnp.concatenate([jnp.where(lo, qt, zero), jnp.where(lo, zero, qt)],
                          axis=1).astype(BF16)
    cols = 2 * L

    km = kmean_ref[...]
    km_hi = km.astype(BF16)
    km_lo = (km - km_hi.astype(F32)).astype(BF16)
    gate = _dot(km_hi, q2t) + _dot(km_lo, q2t)
    rowf = lax.broadcasted_iota(jnp.int32, gate.shape, 0).astype(F32)
    g = jnp.where(rowf < c.astype(F32), gate, NEG)
    sels = []
    for t in range(MOBA_TOPK):
        gmax = jnp.max(g, axis=0, keepdims=True)
        idx = jnp.min(jnp.where(g == gmax, rowf, float(n_blocks)), axis=0, keepdims=True)
        sels.append(jnp.where(t < c, idx, -1.0))
        g = jnp.where(rowf == idx, -jnp.inf, g)

    def selected(jf):
        hit = sels[0] == jf
        for s_t in sels[1:]:
            hit = jnp.logical_or(hit, s_t == jf)
        return hit

    def scores(j, n=1):
        return _dot(k_ref[pl.ds(pl.multiple_of(j * L, L), n * L), :], q2t)

    def colmax(st):
        return jnp.max(st, axis=0, keepdims=True)

    st_own = scores(c) + bown_ref[0]
    m_own = colmax(st_own)
    yield

    def far_scores(j0, n, m_far):
        st = scores(j0, n)
        for g in range(n):
            stg = jnp.where(selected((j0 + g).astype(F32)), st[g * L:(g + 1) * L], NEG)
            st_ref[j0 + g] = stg
            m_far = jnp.maximum(m_far, colmax(stg))
        return m_far

    n_far = c

    def far_sweep(fn, carry):
        start = jnp.int32(0)
        for n in FAR_GROUPS:
            count = lax.shift_right_logical(n_far - start, int(math.log2(n)))
            carry = lax.fori_loop(0, count, lambda i, cr, s=start, n=n: fn(s + i * n, n, cr),
                                  carry)
            start = start + count * n
        return carry

    m_far = far_sweep(far_scores, jnp.full((1, cols), NEG, F32))

    def fix_prev(_):
        tops = []
        for head in range(2):
            cs = slice(head * L, head * L + CORNER)
            blk = (st_ref[c - 1, L - CORNER:, cs]
                   + bcorner_ref[0, :, head * CORNER:(head + 1) * CORNER])
            st_ref[c - 1, L - CORNER:, cs] = blk
            tops += [colmax(blk), jnp.full((1, L - CORNER), NEG, F32)]
        return jnp.concatenate(tops, axis=1)

    m_corner = lax.cond(c > 0, fix_prev, lambda _: jnp.full((1, cols), NEG, F32), 0)
    yield
    bconst = bconst_ref[0]
    m = jnp.maximum(m_own, jnp.maximum(m_far, m_corner) + bconst)

    def add_pv(acc, j, x):
        p = jnp.exp(x.astype(BF16))
        pv = [_dot(vt_ref[j, head], p[:, head * L:(head + 1) * L]) for head in range(2)]
        return pv if acc is None else [a + b for a, b in zip(acc, pv)]

    acc = add_pv(None, c, st_own - m)
    m_off = m - bconst
    yield

    def far_pv(j0, n, acc):
        for g in range(n):
            acc = add_pv(acc, j0 + g, st_ref[j0 + g] - m_off)
        return acc

    acc = far_sweep(far_pv, acc)
    out_t = jnp.concatenate([a[:HEAD_DIM] * (1.0 / a[HEAD_DIM:HEAD_DIM + 1]) for a in acc],
                            axis=0)
    yield out_t.T


def _moba(qkv, bown_t, bcorner_t, bconst, n_batch, seq, q_col, k_col, v_col, n_heads):
    t = qkv.shape[0]
    L = MOBA_BLOCK
    nb = seq // L
    nbp = -(-nb // 8) * 8
    n_pairs = n_heads // 2
    assert nb % MOBA_Q_BLOCKS == 0
    steps = nb // MOBA_Q_BLOCKS
    return pl.pallas_call(
        functools.partial(_moba_kernel, n_blocks=nb),
        grid=(n_batch, n_pairs, steps),
        in_specs=[
            pl.BlockSpec((MOBA_Q_BLOCKS * L, LANES), lambda b, p, c: (b * steps + c, q_col + p)),
            pl.BlockSpec((seq, LANES), lambda b, p, c: (b, k_col + p)),
            pl.BlockSpec((seq, LANES), lambda b, p, c: (b, v_col + p)),
            pl.BlockSpec((1, L, 2 * L), lambda b, p, c: (p, 0, 0)),
            pl.BlockSpec((1, CORNER, 2 * CORNER), lambda b, p, c: (p, 0, 0)),
            pl.BlockSpec((1, 1, 2 * L), lambda b, p, c: (p, 0, 0)),
        ],
        out_specs=pl.BlockSpec((MOBA_Q_BLOCKS * L, LANES), lambda b, p, c: (b * steps + c, p)),
        out_shape=jax.ShapeDtypeStruct((t, n_heads * HEAD_DIM), BF16),
        scratch_shapes=[pltpu.VMEM((nbp, LANES), F32),
                        pltpu.VMEM((nb, 2, HEAD_DIM + ONES_ROWS, L), BF16),
                        pltpu.VMEM((nb, L, 2 * L), F32)],
        compiler_params=_cparams(3),
        name="moba_attention",
    )(qkv, qkv, qkv, bown_t, bcorner_t, bconst)


def _outproj_ffn_kernel(*refs, n_cast):
    x_ref, ya_ref, yb_ref, mod_ref, gg_ref, gpost1_ref, w_ref, gpre2_ref, gpost2_ref = refs[:9]
    w_refs, src_refs = refs[9:12], refs[12:12 + n_cast]
    o_ref, dst_refs = refs[12 + n_cast], refs[13 + n_cast:]
    wa = ya_ref.shape[1]
    mod = mod_ref[0]
    gg = gg_ref[0]
    ya = _rms(ya_ref[...].astype(F32), gg[:, :wa]).astype(BF16)
    yb = _rms(yb_ref[...].astype(F32), gg[:, wa:]).astype(BF16)
    y = _dot(ya, w_ref[:wa, :]) + _dot(yb, w_ref[wa:, :])
    x = x_ref[...] + _rms(y, gpost1_ref[0] * mod[5:6])
    o_ref[...] = _swiglu_step(x, mod, 6, gpre2_ref[0], gpost2_ref[0], w_refs)
    _cast_slabs(src_refs, dst_refs)


def _outproj_ffn(x, ya, yb, mod_l, gg, gpost1, w_out, gpre2, gpost2, w_bf16, cast_sources, seq,
                 *, tm=512):
    t, d = x.shape
    d_ff = w_bf16[0].shape[-1]
    wa, wb = ya.shape[1], yb.shape[1]
    tiles_per_batch = seq // tm
    n_steps = t // tm
    cast_in, cast_out, cast_shape = _cast_plan(cast_sources, n_steps)
    out = pl.pallas_call(
        functools.partial(_outproj_ffn_kernel, n_cast=len(cast_in)),
        grid=(n_steps,),
        in_specs=[
            pl.BlockSpec((tm, d), lambda i: (i, 0)),
            pl.BlockSpec((tm, wa), lambda i: (i, 0)),
            pl.BlockSpec((tm, wb), lambda i: (i, 0)),
            pl.BlockSpec((1, N_MOD, d), lambda i: (i // tiles_per_batch, 0, 0)),
            _vec_spec(wa + wb), _vec_spec(d),
            pl.BlockSpec((wa + wb, d), lambda i: (0, 0), pipeline_mode=pl.Buffered(1)),
            _vec_spec(d), _vec_spec(d),
            *_ffn_specs(d, d_ff),
            *cast_in,
        ],
        out_specs=[pl.BlockSpec((tm, d), lambda i: (i, 0)), *cast_out],
        out_shape=[jax.ShapeDtypeStruct((t, d), F32), *cast_shape],
        compiler_params=_cparams(1),
        name="outproj_swiglu",
    )(x, ya, yb, mod_l, gg, gpost1, w_out, gpre2, gpost2, *w_bf16,
      *[w for w, _ in cast_sources])
    return out[0], tuple(out[1:])


def _t5_bucket_np(dist):
    n = np.maximum(dist, 0)
    max_exact = T5_BUCKETS // 2
    nf = np.maximum(n, 1).astype(np.float32)
    large = max_exact + (np.log(nf / np.float32(max_exact)) / np.float32(math.log(T5_MAX_DIST / max_exact))
                         * np.float32(T5_BUCKETS - max_exact)).astype(np.int32)
    large = np.minimum(large, T5_BUCKETS - 1)
    return np.where(n < max_exact, n, large).astype(np.int32)


def _bias_kernel(tab_ref, *refs, head_base, heads_per_step):
    n = len(refs) // 2
    step = pl.program_id(0)
    for idx_ref, o_ref in zip(refs[:n], refs[n:]):
        idx = idx_ref[...]
        width = idx.shape[1]
        for hh in range(heads_per_step):
            head = head_base + step * heads_per_step + hh
            out = jnp.full(idx.shape, NEG, F32)
            for bucket in range(T5_BUCKETS):
                out = jnp.where(idx == bucket, tab_ref[bucket, head], out)
            o_ref[0, :, hh * width:(hh + 1) * width] = out


def _bias_expand(rel_bias, buckets, head_base, n_steps, heads_per_step):
    shapes = [b.shape for b in buckets]
    return pl.pallas_call(
        functools.partial(_bias_kernel, head_base=head_base, heads_per_step=heads_per_step),
        grid=(n_steps,),
        in_specs=[pl.BlockSpec(memory_space=pltpu.SMEM)]
        + [pl.BlockSpec(s, lambda h: (0, 0)) for s in shapes],
        out_specs=[pl.BlockSpec((1, s[0], heads_per_step * s[1]), lambda h: (h, 0, 0))
                   for s in shapes],
        out_shape=[jax.ShapeDtypeStruct((n_steps, s[0], heads_per_step * s[1]), F32)
                   for s in shapes],
        compiler_params=_cparams(1),
        name="t5_bias_tables",
    )(rel_bias, *[jnp.asarray(b) for b in buckets])


def _bias_tables(rel_bias, n_swa_heads, n_swa_kv_heads):
    n_heads = rel_bias.shape[1]
    kj = np.arange(2 * SWA_BLOCK)[:, None]
    qi = np.arange(SWA_BLOCK)[None, :]
    dist = qi + SWA_BLOCK - kj
    swa_idx = np.where((dist >= 0) & (dist < SWA_WINDOW), _t5_bucket_np(dist), -1).astype(np.int32)
    (swa,) = _bias_expand(rel_bias, [swa_idx], 0, n_swa_kv_heads,
                          n_swa_heads // n_swa_kv_heads)

    kk = np.arange(MOBA_BLOCK)[:, None]
    qq = np.arange(MOBA_BLOCK)[None, :]
    d_own = qq - kk
    own_idx = np.where(d_own >= 0, _t5_bucket_np(d_own), -1).astype(np.int32)
    far = int(_t5_bucket_np(np.array(1 << 30)))
    prev_idx = _t5_bucket_np(d_own + MOBA_BLOCK)
    outside = np.ones_like(prev_idx, dtype=bool)
    outside[MOBA_BLOCK - CORNER:, :CORNER] = False
    assert np.all(prev_idx[outside] == far)
    corner_idx = prev_idx[MOBA_BLOCK - CORNER:, :CORNER]
    n_pairs = (n_heads - n_swa_heads) // 2
    own_t, corner_t = _bias_expand(rel_bias, [own_idx, corner_idx], n_swa_heads, n_pairs, 2)
    const = rel_bias[far, n_swa_heads:]
    corner_t = corner_t - jnp.repeat(const, CORNER).reshape(n_pairs, 1, 2 * CORNER)
    const = jnp.repeat(const, MOBA_BLOCK).reshape(n_pairs, 1, 2 * MOBA_BLOCK)
    return swa, own_t, corner_t, const


def kernel(x, c, rel_bias, ada_w, ada_b, norm_pre, norm_post, ffn_w_gate, ffn_w_up, ffn_w_down,
           mix_w_in, mix_b_in, mix_w_out, attn_sinks, group_gain):
    n_batch, seq, d = x.shape
    depth = ada_w.shape[0]
    n_heads = d // HEAD_DIM
    swa_q_heads = n_heads // 2
    swa_kv_heads = max(1, swa_q_heads // 4)
    moba_heads = n_heads - swa_q_heads
    swa_w = swa_q_heads * HEAD_DIM
    kv_w = swa_kv_heads * HEAD_DIM
    moba_w = moba_heads * HEAD_DIM
    q_col = (swa_w + 2 * kv_w) // LANES
    k_col = q_col + moba_w // LANES
    v_col = k_col + moba_w // LANES

    mod = _modulation(c, ada_w, ada_b).reshape(depth, n_batch, N_MOD, d)
    bias_swa, bias_own_t, bias_corner_t, bias_const = _bias_tables(rel_bias, swa_q_heads, swa_kv_heads)
    sinks = jnp.repeat(attn_sinks, SWA_BLOCK, axis=1).reshape(depth, swa_kv_heads, 1, -1)

    ffn_w = (ffn_w_gate, ffn_w_up, ffn_w_down)
    w_first = tuple(w[0, 0].astype(BF16) for w in ffn_w)
    w_in = mix_w_in[0].astype(BF16)
    b_in = mix_b_in[:, None, :]
    gpre = norm_pre[:, :, None, None, :]
    gpost = norm_post[:, :, None, None, :]
    gg = group_gain[:, None, None, :]

    xt = x.reshape(n_batch * seq, d)
    for l in range(depth):
        xt, (*w_second, w_out) = _ffn(xt, mod[l], gpre[l, 0], gpost[l, 0], w_first,
                                      [(w, (l, 1)) for w in ffn_w] + [(mix_w_out, (l,))], seq)
        qkv = _inproj(xt, mod[l], gpre[l, 1], w_in, b_in, l, seq)
        ya = _swa(qkv, bias_swa, sinks[l], n_batch, seq, swa_q_heads, swa_kv_heads)
        yb = _moba(qkv, bias_own_t, bias_corner_t, bias_const, n_batch, seq, q_col, k_col, v_col,
                   moba_heads)
        nxt = ([(w, (l + 1, 0)) for w in ffn_w] + [(mix_w_in, (l + 1,))]) if l + 1 < depth else []
        xt, nxt = _outproj_ffn(xt, ya, yb, mod[l], gg[l], gpost[l, 1], w_out, gpre[l, 2],
                               gpost[l, 2], w_second, nxt, seq)
        if nxt:
            *w_first, w_in = nxt
    return xt.reshape(n_batch, seq, d)
```

```python
import functools
import math

import numpy as np
import jax
import jax.numpy as jnp
from jax import lax
from jax.experimental import pallas as pl
from jax.experimental.pallas import tpu as pltpu

HEAD_DIM = 64
LANES = 128
SWA_BLOCK = 128
SWA_WINDOW = 128
MOBA_BLOCK = 256
MOBA_TOPK = 3
FAR_GROUPS = (16, 8, 4, 2, 1)
MOBA_Q_BLOCKS = 2
CORNER = 128
BF16_SUBLANES = 16
ONES_ROWS = BF16_SUBLANES
T5_BUCKETS = 32
T5_MAX_DIST = 128
N_MOD = 9
EPS = 1e-6
NEG = -1e30
VMEM_LIMIT = 56 * 1024 * 1024

F32 = jnp.float32
BF16 = jnp.bfloat16


def _cparams(n_axes):
    return pltpu.CompilerParams(
        dimension_semantics=("arbitrary",) * n_axes,
        vmem_limit_bytes=VMEM_LIMIT,
    )


def _rms(x, g):
    return x * lax.rsqrt(jnp.mean(x * x, axis=-1, keepdims=True) + EPS) * g


def _dot(a, b):
    return jnp.dot(a, b, preferred_element_type=F32)


def _dot_nt(a, b):
    return lax.dot_general(a, b, (((1,), (1,)), ((), ())), preferred_element_type=F32)


def _mod_kernel(ct_ref, w_ref, b_ref, o_ref, *, n_batch, row_chunk):
    ct = ct_ref[...]
    ca = ct * jax.nn.sigmoid(ct)
    d = ct.shape[0]
    for b in range(n_batch):
        acc = b_ref[0]
        for r in range(0, d, row_chunk):
            w = w_ref[0, r:r + row_chunk, :]
            acc = acc + jnp.sum(w * ca[r:r + row_chunk, b:b + 1], axis=0, keepdims=True)
        o_ref[0, b:b + 1, :] = acc


def _modulation(c, ada_w, ada_b, *, tn=1152, row_chunk=128):
    depth, d, n = ada_w.shape
    n_batch = c.shape[0]
    return pl.pallas_call(
        functools.partial(_mod_kernel, n_batch=n_batch, row_chunk=row_chunk),
        grid=(depth, n // tn),
        in_specs=[
            pl.BlockSpec((d, n_batch), lambda l, j: (0, 0)),
            pl.BlockSpec((1, d, tn), lambda l, j: (l, 0, j)),
            pl.BlockSpec((1, 1, tn), lambda l, j: (l, 0, j)),
        ],
        out_specs=pl.BlockSpec((1, n_batch, tn), lambda l, j: (l, 0, j)),
        out_shape=jax.ShapeDtypeStruct((depth, n_batch, n), F32),
        compiler_params=_cparams(2),
        name="adaln_modulation",
    )(c.T, ada_w, ada_b.reshape(depth, 1, n))


def _modulate(x, gpre, mod, base):
    return (_rms(x, gpre * (1.0 + mod[base + 1:base + 2])) + mod[base:base + 1]).astype(BF16)


def _swiglu_step(x, mod, base, gpre, gpost, w_refs):
    wg_ref, wu_ref, wd_ref = w_refs
    h = _modulate(x, gpre, mod, base)
    g = _dot(h, wg_ref[...])
    u = _dot(h, wu_ref[...])
    y = _dot((g * jax.nn.sigmoid(g) * u).astype(BF16), wd_ref[...])
    return x + _rms(y, gpost * (0.5 * mod[base + 2:base + 3]))


def _ffn_specs(d, d_ff):
    const = pl.Buffered(1)
    return [
        pl.BlockSpec((d, d_ff), lambda i: (0, 0), pipeline_mode=const),
        pl.BlockSpec((d, d_ff), lambda i: (0, 0), pipeline_mode=const),
        pl.BlockSpec((d_ff, d), lambda i: (0, 0), pipeline_mode=const),
    ]


def _vec_spec(width):
    return pl.BlockSpec((1, 1, width), lambda i: (0, 0, 0))


def _slab_rows(n_rows, n_steps):
    return min(r for r in range(BF16_SUBLANES, n_rows + 1, BF16_SUBLANES)
               if n_rows % r == 0 and r * n_steps >= n_rows)


def _cast_plan(sources, n_steps):
    in_specs, out_specs, out_shape = [], [], []
    for w, lead in sources:
        n_rows, n_cols = w.shape[-2:]
        rows = _slab_rows(n_rows, n_steps)

        def slab(i, last=n_rows // rows - 1):
            return jnp.minimum(i, last)

        in_specs.append(pl.BlockSpec((1,) * len(lead) + (rows, n_cols),
                                     lambda i, slab=slab, lead=lead: (*lead, slab(i), 0)))
        out_specs.append(pl.BlockSpec((rows, n_cols), lambda i, slab=slab: (slab(i), 0)))
        out_shape.append(jax.ShapeDtypeStruct((n_rows, n_cols), BF16))
    return in_specs, out_specs, out_shape


def _cast_slabs(src_refs, dst_refs):
    for src, dst in zip(src_refs, dst_refs):
        dst[...] = src[...].reshape(dst.shape).astype(dst.dtype)


def _ffn_kernel(*refs, n_cast):
    x_ref, mod_ref, gpre_ref, gpost_ref = refs[:4]
    w_refs, src_refs = refs[4:7], refs[7:7 + n_cast]
    o_ref, dst_refs = refs[7 + n_cast], refs[8 + n_cast:]
    o_ref[...] = _swiglu_step(x_ref[...], mod_ref[0], 0, gpre_ref[0], gpost_ref[0], w_refs)
    _cast_slabs(src_refs, dst_refs)


def _ffn(x, mod_l, gpre, gpost, w_bf16, cast_sources, seq, *, tm=512):
    t, d = x.shape
    d_ff = w_bf16[0].shape[-1]
    tiles_per_batch = seq // tm
    n_steps = t // tm
    cast_in, cast_out, cast_shape = _cast_plan(cast_sources, n_steps)
    out = pl.pallas_call(
        functools.partial(_ffn_kernel, n_cast=len(cast_in)),
        grid=(n_steps,),
        in_specs=[
            pl.BlockSpec((tm, d), lambda i: (i, 0)),
            pl.BlockSpec((1, N_MOD, d), lambda i: (i // tiles_per_batch, 0, 0)),
            _vec_spec(d), _vec_spec(d),
            *_ffn_specs(d, d_ff),
            *cast_in,
        ],
        out_specs=[pl.BlockSpec((tm, d), lambda i: (i, 0)), *cast_out],
        out_shape=[jax.ShapeDtypeStruct((t, d), F32), *cast_shape],
        compiler_params=_cparams(1),
        name="swiglu_half_step",
    )(x, mod_l, gpre, gpost, *w_bf16, *[w for w, _ in cast_sources])
    return out[0], tuple(out[1:])


def _inproj_kernel(x_ref, mod_ref, gpre_ref, w_ref, b_ref, o_ref):
    h = _modulate(x_ref[...], gpre_ref[0], mod_ref[0], 3)
    o_ref[...] = (_dot(h, w_ref[...]) + b_ref[0]).astype(o_ref.dtype)


def _inproj(x, mod_l, gpre, w_in, b_in, layer, seq, *, tm=512):
    t, d = x.shape
    n = w_in.shape[-1]
    tiles_per_batch = seq // tm
    return pl.pallas_call(
        _inproj_kernel,
        grid=(t // tm,),
        in_specs=[
            pl.BlockSpec((tm, d), lambda i: (i, 0)),
            pl.BlockSpec((1, N_MOD, d), lambda i: (i // tiles_per_batch, 0, 0)),
            _vec_spec(d),
            pl.BlockSpec((d, n), lambda i: (0, 0), pipeline_mode=pl.Buffered(1)),
            pl.BlockSpec((1, 1, n), lambda i: (layer, 0, 0)),
        ],
        out_specs=pl.BlockSpec((tm, n), lambda i: (i, 0)),
        out_shape=jax.ShapeDtypeStruct((t, n), BF16),
        compiler_params=_cparams(1),
        name="qkv_projection",
    )(x, mod_l, gpre, w_in, b_in)


def _swa_kernel(q_ref, kp_ref, km_ref, vp_ref, vm_ref, bias_ref, sink_ref, o_ref, *, group):
    tq = q_ref.shape[0]
    W = 2 * SWA_BLOCK
    i = pl.program_id(1)
    k_all = jnp.concatenate([kp_ref[...], km_ref[...]], axis=0)
    v_all = jnp.concatenate([vp_ref[...], vm_ref[...]], axis=0)
    vt_all = v_all.astype(F32).T.astype(BF16)
    ones = jnp.ones((ONES_ROWS, v_all.shape[0]), BF16)
    vt = [jnp.concatenate([vt_all[kv * HEAD_DIM:(kv + 1) * HEAD_DIM], ones], axis=0)
          for kv in range(2)]
    feat_lo = lax.broadcasted_iota(jnp.int32, (LANES, SWA_BLOCK), 0) < HEAD_DIM
    key_row = lax.broadcasted_iota(jnp.int32, (W, group * SWA_BLOCK), 0)
    no_prev = jnp.logical_and(key_row < SWA_BLOCK, i == 0)
    scale = HEAD_DIM ** -0.5

    def swap_halves(x):
        return jnp.concatenate([x[HEAD_DIM:], x[:HEAD_DIM]], axis=0)

    for blk in range(tq // SWA_BLOCK):
        r0 = blk * SWA_BLOCK
        kw = k_all[r0:r0 + W]
        for kv in range(2):
            cols = []
            for g in range(group):
                h = kv * group + g
                pair, odd = h // 2, h % 2
                qpt = (q_ref[r0:r0 + SWA_BLOCK, pair * LANES:(pair + 1) * LANES].astype(F32)
                       * scale).T
                if odd != kv:
                    qpt = swap_halves(qpt)
                keep = feat_lo if kv == 0 else jnp.logical_not(feat_lo)
                cols.append(jnp.where(keep, qpt, jnp.zeros_like(qpt)))
            qt = jnp.concatenate(cols, axis=1).astype(BF16)
            st = _dot(kw, qt)
            st = st + bias_ref[kv]
            if blk == 0:
                st = jnp.where(no_prev, NEG, st)
            sink = sink_ref[kv]
            m = jnp.maximum(jnp.max(st, axis=0, keepdims=True), sink)
            acc = _dot(vt[kv][:, r0:r0 + W], jnp.exp(st - m).astype(BF16))
            out_t = acc[:HEAD_DIM] * (1.0 / (acc[HEAD_DIM:HEAD_DIM + 1] + jnp.exp(sink - m)))
            for gp in range(group // 2):
                c0 = 2 * gp * SWA_BLOCK
                pair_t = jnp.concatenate(
                    [out_t[:, c0:c0 + SWA_BLOCK], out_t[:, c0 + SWA_BLOCK:c0 + 2 * SWA_BLOCK]],
                    axis=0)
                pair = (kv * group) // 2 + gp
                o_ref[r0:r0 + SWA_BLOCK, pair * LANES:(pair + 1) * LANES] = pair_t.T.astype(
                    o_ref.dtype)


def _swa(qkv, bias, sinks, n_batch, seq, n_q_heads, n_kv_heads, *, tq=1024):
    t = qkv.shape[0]
    nq = seq // tq
    qw = n_q_heads * HEAD_DIM
    k_col = qw // LANES
    v_col = k_col + (n_kv_heads * HEAD_DIM) // LANES
    per = tq // SWA_BLOCK
    group = n_q_heads // n_kv_heads
    assert n_kv_heads * HEAD_DIM == LANES and group % 2 == 0

    def prev_map(col):
        return lambda b, i: (jnp.maximum((b * nq + i) * per - 1, 0), col)

    return pl.pallas_call(
        functools.partial(_swa_kernel, group=group),
        grid=(n_batch, nq),
        in_specs=[
            pl.BlockSpec((tq, qw), lambda b, i: (b * nq + i, 0)),
            pl.BlockSpec((SWA_BLOCK, LANES), prev_map(k_col)),
            pl.BlockSpec((tq, LANES), lambda b, i: (b * nq + i, k_col)),
            pl.BlockSpec((SWA_BLOCK, LANES), prev_map(v_col)),
            pl.BlockSpec((tq, LANES), lambda b, i: (b * nq + i, v_col)),
            pl.BlockSpec((n_kv_heads, 2 * SWA_BLOCK, group * SWA_BLOCK), lambda b, i: (0, 0, 0)),
            pl.BlockSpec((n_kv_heads, 1, group * SWA_BLOCK), lambda b, i: (0, 0, 0)),
        ],
        out_specs=pl.BlockSpec((tq, qw), lambda b, i: (b * nq + i, 0)),
        out_shape=jax.ShapeDtypeStruct((t, qw), BF16),
        compiler_params=_cparams(2),
        name="swa_attention",
    )(qkv, qkv, qkv, qkv, qkv, bias, sinks)


def _moba_kernel(q_ref, k_ref, v_ref, bown_ref, bcorner_ref, bconst_ref, o_ref, kmean_ref, vt_ref,
                 st_ref, *, n_blocks):
    L = MOBA_BLOCK
    step = pl.program_id(2)

    @pl.when(step == 0)
    def _():
        kmean_ref[...] = jnp.zeros_like(kmean_ref)
        for jb in range(n_blocks):
            kb = k_ref[jb * L:(jb + 1) * L, :].astype(F32)
            kmean_ref[jb:jb + 1, :] = jnp.sum(kb, axis=0, keepdims=True) * (1.0 / L)
            vt = v_ref[jb * L:(jb + 1) * L, :].astype(F32).T.astype(BF16)
            for head in range(2):
                vt_ref[jb, head, :HEAD_DIM, :] = vt[head * HEAD_DIM:(head + 1) * HEAD_DIM]
                vt_ref[jb, head, HEAD_DIM:, :] = jnp.ones((ONES_ROWS, L), BF16)

    blocks = [_moba_query_block(step * MOBA_Q_BLOCKS + sub, q_ref[sub * L:(sub + 1) * L, :],
                                k_ref, bown_ref, bcorner_ref, bconst_ref, kmean_ref, vt_ref,
                                st_ref, n_blocks) for sub in range(MOBA_Q_BLOCKS)]
    for blk in blocks:
        next(blk)
    for sub, blk in enumerate(blocks):
        next(blk)
        next(blk)
        o_ref[sub * L:(sub + 1) * L, :] = next(blk).astype(o_ref.dtype)


def _moba_query_block(c, q, k_ref, bown_ref, bcorner_ref, bconst_ref, kmean_ref, vt_ref, st_ref,
                      n_blocks):
    L = MOBA_BLOCK
    scale = HEAD_DIM ** -0.5
    qt = (q.astype(F32) * scale).T
    lo = lax.broadcasted_iota(jnp.int32, (LANES, L), 0) < HEAD_DIM
    zero = jnp.zeros_like(qt)
    q2t = jnp.concatenate([jnp.where(lo, qt, zero), jnp.where(lo, zero, qt)],
                          axis=1).astype(BF16)
    cols = 2 * L

    km = kmean_ref[...]
    km_hi = km.astype(BF16)
    km_lo = (km - km_hi.astype(F32)).astype(BF16)
    gate = _dot(km_hi, q2t) + _dot(km_lo, q2t)
    rowf = lax.broadcasted_iota(jnp.int32, gate.shape, 0).astype(F32)
    g = jnp.where(rowf < c.astype(F32), gate, NEG)
    sels = []
    for t in range(MOBA_TOPK):
        gmax = jnp.max(g, axis=0, keepdims=True)
        idx = jnp.min(jnp.where(g == gmax, rowf, float(n_blocks)), axis=0, keepdims=True)
        sels.append(jnp.where(t < c, idx, -1.0))
        g = jnp.where(rowf == idx, -jnp.inf, g)

    def selected(jf):
        hit = sels[0] == jf
        for s_t in sels[1:]:
            hit = jnp.logical_or(hit, s_t == jf)
        return hit

    def scores(j, n=1):
        return _dot(k_ref[pl.ds(pl.multiple_of(j * L, L), n * L), :], q2t)

    def colmax(st):
        return jnp.max(st, axis=0, keepdims=True)

    st_own = scores(c) + bown_ref[0]
    m_own = colmax(st_own)
    yield

    def far_scores(j0, n, m_far):
        st = scores(j0, n)
        for g in range(n):
            stg = jnp.where(selected((j0 + g).astype(F32)), st[g * L:(g + 1) * L], NEG)
            st_ref[j0 + g] = stg
            m_far = jnp.maximum(m_far, colmax(stg))
        return m_far

    n_far = c

    def far_sweep(fn, carry):
        start = jnp.int32(0)
        for n in FAR_GROUPS:
            count = lax.shift_right_logical(n_far - start, int(math.log2(n)))
            carry = lax.fori_loop(0, count, lambda i, cr, s=start, n=n: fn(s + i * n, n, cr),
                                  carry)
            start = start + count * n
        return carry

    m_far = far_sweep(far_scores, jnp.full((1, cols), NEG, F32))

    def fix_prev(_):
        tops = []
        for head in range(2):
            cs = slice(head * L, head * L + CORNER)
            blk = (st_ref[c - 1, L - CORNER:, cs]
                   + bcorner_ref[0, :, head * CORNER:(head + 1) * CORNER])
            st_ref[c - 1, L - CORNER:, cs] = blk
            tops += [colmax(blk), jnp.full((1, L - CORNER), NEG, F32)]
        return jnp.concatenate(tops, axis=1)

    m_corner = lax.cond(c > 0, fix_prev, lambda _: jnp.full((1, cols), NEG, F32), 0)
    yield
    bconst = bconst_ref[0]
    m = jnp.maximum(m_own, jnp.maximum(m_far, m_corner) + bconst)

    def add_pv(acc, j, x):
        p = jnp.exp(x.astype(BF16))
        pv = [_dot(vt_ref[j, head], p[:, head * L:(head + 1) * L]) for head in range(2)]
        return pv if acc is None else [a + b for a, b in zip(acc, pv)]

    acc = add_pv(None, c, st_own - m)
    m_off = m - bconst
    yield

    def far_pv(j0, n, acc):
        for g in range(n):
            acc = add_pv(acc, j0 + g, st_ref[j0 + g] - m_off)
        return acc

    acc = far_sweep(far_pv, acc)
    out_t = jnp.concatenate([a[:HEAD_DIM] * (1.0 / a[HEAD_DIM:HEAD_DIM + 1]) for a in acc],
                            axis=0)
    yield out_t.T


def _moba(qkv, bown_t, bcorner_t, bconst, n_batch, seq, q_col, k_col, v_col, n_heads):
    t = qkv.shape[0]
    L = MOBA_BLOCK
    nb = seq // L
    nbp = -(-nb // 8) * 8
    n_pairs = n_heads // 2
    assert nb % MOBA_Q_BLOCKS == 0
    steps = nb // MOBA_Q_BLOCKS
    return pl.pallas_call(
        functools.partial(_moba_kernel, n_blocks=nb),
        grid=(n_batch, n_pairs, steps),
        in_specs=[
            pl.BlockSpec((MOBA_Q_BLOCKS * L, LANES), lambda b, p, c: (b * steps + c, q_col + p)),
            pl.BlockSpec((seq, LANES), lambda b, p, c: (b, k_col + p)),
            pl.BlockSpec((seq, LANES), lambda b, p, c: (b, v_col + p)),
            pl.BlockSpec((1, L, 2 * L), lambda b, p, c: (p, 0, 0)),
            pl.BlockSpec((1, CORNER, 2 * CORNER), lambda b, p, c: (p, 0, 0)),
            pl.BlockSpec((1, 1, 2 * L), lambda b, p, c: (p, 0, 0)),
        ],
        out_specs=pl.BlockSpec((MOBA_Q_BLOCKS * L, LANES), lambda b, p, c: (b * steps + c, p)),
        out_shape=jax.ShapeDtypeStruct((t, n_heads * HEAD_DIM), BF16),
        scratch_shapes=[pltpu.VMEM((nbp, LANES), F32),
                        pltpu.VMEM((nb, 2, HEAD_DIM + ONES_ROWS, L), BF16),
                        pltpu.VMEM((nb, L, 2 * L), F32)],
        compiler_params=_cparams(3),
        name="moba_attention",
    )(qkv, qkv, qkv, bown_t, bcorner_t, bconst)


def _outproj_ffn_kernel(*refs, n_cast):
    x_ref, ya_ref, yb_ref, mod_ref, gg_ref, gpost1_ref, w_ref, gpre2_ref, gpost2_ref = refs[:9]
    w_refs, src_refs = refs[9:12], refs[12:12 + n_cast]
    o_ref, dst_refs = refs[12 + n_cast], refs[13 + n_cast:]
    wa = ya_ref.shape[1]
    mod = mod_ref[0]
    gg = gg_ref[0]
    ya = _rms(ya_ref[...].astype(F32), gg[:, :wa]).astype(BF16)
    yb = _rms(yb_ref[...].astype(F32), gg[:, wa:]).astype(BF16)
    y = _dot(ya, w_ref[:wa, :]) + _dot(yb, w_ref[wa:, :])
    x = x_ref[...] + _rms(y, gpost1_ref[0] * mod[5:6])
    o_ref[...] = _swiglu_step(x, mod, 6, gpre2_ref[0], gpost2_ref[0], w_refs)
    _cast_slabs(src_refs, dst_refs)


def _outproj_ffn(x, ya, yb, mod_l, gg, gpost1, w_out, gpre2, gpost2, w_bf16, cast_sources, seq,
                 *, tm=512):
    t, d = x.shape
    d_ff = w_bf16[0].shape[-1]
    wa, wb = ya.shape[1], yb.shape[1]
    tiles_per_batch = seq // tm
    n_steps = t // tm
    cast_in, cast_out, cast_shape = _cast_plan(cast_sources, n_steps)
    out = pl.pallas_call(
        functools.partial(_outproj_ffn_kernel, n_cast=len(cast_in)),
        grid=(n_steps,),
        in_specs=[
            pl.BlockSpec((tm, d), lambda i: (i, 0)),
            pl.BlockSpec((tm, wa), lambda i: (i, 0)),
            pl.BlockSpec((tm, wb), lambda i: (i, 0)),
            pl.BlockSpec((1, N_MOD, d), lambda i: (i // tiles_per_batch, 0, 0)),
            _vec_spec(wa + wb), _vec_spec(d),
            pl.BlockSpec((wa + wb, d), lambda i: (0, 0), pipeline_mode=pl.Buffered(1)),
            _vec_spec(d), _vec_spec(d),
            *_ffn_specs(d, d_ff),
            *cast_in,
        ],
        out_specs=[pl.BlockSpec((tm, d), lambda i: (i, 0)), *cast_out],
        out_shape=[jax.ShapeDtypeStruct((t, d), F32), *cast_shape],
        compiler_params=_cparams(1),
        name="outproj_swiglu",
    )(x, ya, yb, mod_l, gg, gpost1, w_out, gpre2, gpost2, *w_bf16,
      *[w for w, _ in cast_sources])
    return out[0], tuple(out[1:])


def _t5_bucket_np(dist):
    n = np.maximum(dist, 0)
    max_exact = T5_BUCKETS // 2
    nf = np.maximum(n, 1).astype(np.float32)
    large = max_exact + (np.log(nf / np.float32(max_exact)) / np.float32(math.log(T5_MAX_DIST / max_exact))
                         * np.float32(T5_BUCKETS - max_exact)).astype(np.int32)
    large = np.minimum(large, T5_BUCKETS - 1)
    return np.where(n < max_exact, n, large).astype(np.int32)


def _bias_kernel(tab_ref, *refs, head_base, heads_per_step):
    n = len(refs) // 2
    step = pl.program_id(0)
    for idx_ref, o_ref in zip(refs[:n], refs[n:]):
        idx = idx_ref[...]
        width = idx.shape[1]
        for hh in range(heads_per_step):
            head = head_base + step * heads_per_step + hh
            out = jnp.full(idx.shape, NEG, F32)
            for bucket in range(T5_BUCKETS):
                out = jnp.where(idx == bucket, tab_ref[bucket, head], out)
            o_ref[0, :, hh * width:(hh + 1) * width] = out


def _bias_expand(rel_bias, buckets, head_base, n_steps, heads_per_step):
    shapes = [b.shape for b in buckets]
    return pl.pallas_call(
        functools.partial(_bias_kernel, head_base=head_base, heads_per_step=heads_per_step),
        grid=(n_steps,),
        in_specs=[pl.BlockSpec(memory_space=pltpu.SMEM)]
        + [pl.BlockSpec(s, lambda h: (0, 0)) for s in shapes],
        out_specs=[pl.BlockSpec((1, s[0], heads_per_step * s[1]), lambda h: (h, 0, 0))
                   for s in shapes],
        out_shape=[jax.ShapeDtypeStruct((n_steps, s[0], heads_per_step * s[1]), F32)
                   for s in shapes],
        compiler_params=_cparams(1),
        name="t5_bias_tables",
    )(rel_bias, *[jnp.asarray(b) for b in buckets])


def _bias_tables(rel_bias, n_swa_heads, n_swa_kv_heads):
    n_heads = rel_bias.shape[1]
    kj = np.arange(2 * SWA_BLOCK)[:, None]
    qi = np.arange(SWA_BLOCK)[None, :]
    dist = qi + SWA_BLOCK - kj
    swa_idx = np.where((dist >= 0) & (dist < SWA_WINDOW), _t5_bucket_np(dist), -1).astype(np.int32)
    (swa,) = _bias_expand(rel_bias, [swa_idx], 0, n_swa_kv_heads,
                          n_swa_heads // n_swa_kv_heads)

    kk = np.arange(MOBA_BLOCK)[:, None]
    qq = np.arange(MOBA_BLOCK)[None, :]
    d_own = qq - kk
    own_idx = np.where(d_own >= 0, _t5_bucket_np(d_own), -1).astype(np.int32)
    far = int(_t5_bucket_np(np.array(1 << 30)))
    prev_idx = _t5_bucket_np(d_own + MOBA_BLOCK)
    outside = np.ones_like(prev_idx, dtype=bool)
    outside[MOBA_BLOCK - CORNER:, :CORNER] = False
    assert np.all(prev_idx[outside] == far)
    corner_idx = prev_idx[MOBA_BLOCK - CORNER:, :CORNER]
    n_pairs = (n_heads - n_swa_heads) // 2
    own_t, corner_t = _bias_expand(rel_bias, [own_idx, corner_idx], n_swa_heads, n_pairs, 2)
    const = rel_bias[far, n_swa_heads:]
    corner_t = corner_t - jnp.repeat(const, CORNER).reshape(n_pairs, 1, 2 * CORNER)
    const = jnp.repeat(const, MOBA_BLOCK).reshape(n_pairs, 1, 2 * MOBA_BLOCK)
    return swa, own_t, corner_t, const


def kernel(x, c, rel_bias, ada_w, ada_b, norm_pre, norm_post, ffn_w_gate, ffn_w_up, ffn_w_down,
           mix_w_in, mix_b_in, mix_w_out, attn_sinks, group_gain):
    n_batch, seq, d = x.shape
    depth = ada_w.shape[0]
    n_heads = d // HEAD_DIM
    swa_q_heads = n_heads // 2
    swa_kv_heads = max(1, swa_q_heads // 4)
    moba_heads = n_heads - swa_q_heads
    swa_w = swa_q_heads * HEAD_DIM
    kv_w = swa_kv_heads * HEAD_DIM
    moba_w = moba_heads * HEAD_DIM
    q_col = (swa_w + 2 * kv_w) // LANES
    k_col = q_col + moba_w // LANES
    v_col = k_col + moba_w // LANES

    mod = _modulation(c, ada_w, ada_b).reshape(depth, n_batch, N_MOD, d)
    bias_swa, bias_own_t, bias_corner_t, bias_const = _bias_tables(rel_bias, swa_q_heads, swa_kv_heads)
    sinks = jnp.repeat(attn_sinks, SWA_BLOCK, axis=1).reshape(depth, swa_kv_heads, 1, -1)

    ffn_w = (ffn_w_gate, ffn_w_up, ffn_w_down)
    w_first = tuple(w[0, 0].astype(BF16) for w in ffn_w)
    w_in = mix_w_in[0].astype(BF16)
    b_in = mix_b_in[:, None, :]
    gpre = norm_pre[:, :, None, None, :]
    gpost = norm_post[:, :, None, None, :]
    gg = group_gain[:, None, None, :]

    xt = x.reshape(n_batch * seq, d)
    for l in range(depth):
        xt, (*w_second, w_out) = _ffn(xt, mod[l], gpre[l, 0], gpost[l, 0], w_first,
                                      [(w, (l, 1)) for w in ffn_w] + [(mix_w_out, (l,))], seq)
        qkv = _inproj(xt, mod[l], gpre[l, 1], w_in, b_in, l, seq)
        ya = _swa(qkv, bias_swa, sinks[l], n_batch, seq, swa_q_heads, swa_kv_heads)
        yb = _moba(qkv, bias_own_t, bias_corner_t, bias_const, n_batch, seq, q_col, k_col, v_col,
                   moba_heads)
        nxt = ([(w, (l + 1, 0)) for w in ffn_w] + [(mix_w_in, (l + 1,))]) if l + 1 < depth else []
        xt, nxt = _outproj_ffn(xt, ya, yb, mod[l], gg[l], gpost[l, 1], w_out, gpre[l, 2],
                               gpost[l, 2], w_second, nxt, seq)
        if nxt:
            *w_first, w_in = nxt
    return xt.reshape(n_batch, seq, d)
```
